```python
import math
import jax, jax.numpy as jnp
from jax import lax
import numpy as np

D_MODEL = 1024
BATCH = 4
SEQ = 4096
DEPTH = 4
DEC_BATCH = 128
DEC_SEQ = 8
PAST_LEN = 2048
PAGE_SIZE = 128

N_A_LAYERS = DEPTH // 2
N_B_LAYERS = DEPTH - N_A_LAYERS
D_RNN = (D_MODEL * 5) // 4
RG_BLOCKS = 16
RG_BLOCK_W = D_RNN // RG_BLOCKS
RG_CONV_W = 4
RG_C = 8.0
N_HEADS = 16
HEAD_DIM = 64
Q_BLOCK = 128
D_FF = 3 * D_MODEL
FFN_CONV_W = 3
EPS = 1e-6

kernel_name = 'yoco_rglru_fox_convffn_step'


def _rmsnorm(x, g):
    x32 = x.astype(jnp.float32)
    n = x32 * lax.rsqrt(jnp.mean(x32 * x32, axis=-1, keepdims=True) + EPS)
    return n * g.astype(jnp.float32)


def _modnorm(x, g, shift, scale):
    n = _rmsnorm(x, g) * (1.0 + scale.astype(jnp.float32)[:, None, :]) + shift.astype(jnp.float32)[:, None, :]
    return n.astype(x.dtype)


def _causal_dwconv(x, buf, w, b):
    width = w.shape[0]
    t = x.shape[1]
    xp = jnp.concatenate([buf.astype(x.dtype), x], axis=1)
    y = xp[:, 0:t] * w[0]
    for k in range(1, width):
        y = y + xp[:, k:k + t] * w[k]
    return y + b, xp[:, t:]


def _lin_combine(left, right):
    a1, b1 = left
    a2, b2 = right
    return a1 * a2, a2 * b1 + b2


def _rglru_block(h, conv_buf, h0, w_in, conv_w, conv_b, wa, ba, wx, bx, lam, w_out):
    bsz, t, _ = h.shape
    u = h @ w_in
    gate_br, x_br = jnp.split(u, 2, axis=-1)
    xc, conv_new = _causal_dwconv(x_br, conv_buf, conv_w, conv_b)
    xb = xc.reshape(bsz, t, RG_BLOCKS, RG_BLOCK_W)
    r = jax.nn.sigmoid(jnp.einsum('bthi,hij->bthj', xb, wa).reshape(bsz, t, D_RNN).astype(jnp.float32) + ba.astype(jnp.float32))
    i = jax.nn.sigmoid(jnp.einsum('bthi,hij->bthj', xb, wx).reshape(bsz, t, D_RNN).astype(jnp.float32) + bx.astype(jnp.float32))
    log_a = -RG_C * r * jax.nn.softplus(-lam.astype(jnp.float32))
    a = jnp.exp(log_a)
    b = jnp.sqrt(-jnp.expm1(2.0 * log_a)) * (i * xc.astype(jnp.float32))
    b = b.at[:, 0].add(a[:, 0] * h0.astype(jnp.float32))
    _, hs = lax.associative_scan(_lin_combine, (a, b), axis=1)
    y = (jax.nn.gelu(gate_br) * hs.astype(h.dtype)) @ w_out
    return y, conv_new, hs[:, -1]


def _conv_ffn(h, buf, w_up, conv_w, conv_b, w_down):
    u = h @ w_up
    uc, buf_new = _causal_dwconv(u, buf, conv_w, conv_b)
    g, v = jnp.split(uc, 2, axis=-1)
    return (jax.nn.gelu(g) * v) @ w_down, buf_new


def _forgetting_attention(q, k, v, cq, ck, q_pos, k_pos):
    bsz, t, nh, hd = q.shape
    qb = min(Q_BLOCK, t)
    nblk = -(-t // qb)
    pad = nblk * qb - t
    qp = jnp.pad(q, ((0, 0), (0, pad), (0, 0), (0, 0)))
    cqp = jnp.pad(cq, ((0, 0), (0, pad), (0, 0)))
    posp = jnp.pad(q_pos, (0, pad), mode='edge')
    q_blocks = jnp.moveaxis(qp.reshape(bsz, nblk, qb, nh, hd), 1, 0)
    cq_blocks = jnp.moveaxis(cqp.reshape(bsz, nblk, qb, nh), 1, 0)
    pos_blocks = posp.reshape(nblk, qb)
    ck_t = jnp.transpose(ck, (0, 2, 1))[:, :, None, :]
    scale = 1.0 / math.sqrt(hd)

    def one_block(args):
        qblk, cqblk, pblk = args
        s = jnp.einsum('bqhd,bkhd->bhqk', qblk, k, preferred_element_type=jnp.float32) * scale
        s = s + jnp.transpose(cqblk, (0, 2, 1))[..., None] - ck_t
        mask = k_pos[None, :] <= pblk[:, None]
        s = jnp.where(mask, s, -jnp.inf)
        p = jax.nn.softmax(s, axis=-1)
        return jnp.einsum('bhqk,bkhd->bqhd', p.astype(v.dtype), v)

    o = lax.map(one_block, (q_blocks, cq_blocks, pos_blocks))
    return jnp.moveaxis(o, 0, 1).reshape(bsz, nblk * qb, nh, hd)[:, :t]


def _trunk(x, c, k_past, v_past, lf_past, rg_conv_buf, rg_h0, ffn_buf, p):
    bsz, t, _ = x.shape
    past = k_past.shape[1]
    q_pos = past + jnp.arange(t, dtype=jnp.int32)
    k_pos = jnp.arange(past + t, dtype=jnp.int32)
    c_act = jax.nn.silu(c)
    hw = N_HEADS * HEAD_DIM
    rg_conv_new, rg_h_new, ffn_new = [], [], []
    for l in range(DEPTH):
        mod = c_act @ p['mod_w'][l] + p['mod_b'][l]
        sh1, sc1, g1, sh2, sc2, g2 = jnp.split(mod, 6, axis=-1)
        hn = _modnorm(x, p['norm_mix_g'][l], sh1, sc1)
        if l < N_A_LAYERS:
            out, cb, hl = _rglru_block(hn, rg_conv_buf[l], rg_h0[l], p['rg_w_in'][l], p['rg_conv_w'][l],
                                       p['rg_conv_b'][l], p['rg_wa'][l], p['rg_ba'][l], p['rg_wx'][l],
                                       p['rg_bx'][l], p['rg_lambda'][l], p['rg_w_out'][l])
            rg_conv_new.append(cb)
            rg_h_new.append(hl)
        else:
            j = l - N_A_LAYERS
            q = (hn @ p['fa_wq'][j]).reshape(bsz, t, N_HEADS, HEAD_DIM)
            o = _forgetting_attention(q, k_all, v_all, cq, cum, q_pos, k_pos)
            out = o.reshape(bsz, t, hw) @ p['fa_wo'][j]
        x = x + ((1.0 + g1)[:, None, :] * out).astype(x.dtype)
        hn = _modnorm(x, p['norm_ffn_g'][l], sh2, sc2)
        out, fb = _conv_ffn(hn, ffn_buf[l], p['ffn_w_up'][l], p['ffn_conv_w'][l], p['ffn_conv_b'][l], p['ffn_w_down'][l])
        ffn_new.append(fb)
        x = x + ((1.0 + g2)[:, None, :] * out).astype(x.dtype)
        if l == N_A_LAYERS - 1:
            kv_shift, kv_scale = jnp.split(c_act @ p['kv_mod_w'] + p['kv_mod_b'], 2, axis=-1)
            hk = _modnorm(x, p['kv_norm_g'], kv_shift, kv_scale)
            proj = hk @ p['kv_w']
            k_new = proj[..., :hw].reshape(bsz, t, N_HEADS, HEAD_DIM)
            v_new = proj[..., hw:2 * hw].reshape(bsz, t, N_HEADS, HEAD_DIM)
            lf_new = jax.nn.log_sigmoid(proj[..., 2 * hw:].astype(jnp.float32) + p['kv_b_f'].astype(jnp.float32))
            k_all = jnp.concatenate([k_past.astype(k_new.dtype), k_new], axis=1)
            v_all = jnp.concatenate([v_past.astype(v_new.dtype), v_new], axis=1)
            cum = jnp.cumsum(jnp.concatenate([lf_past.astype(jnp.float32), lf_new], axis=1), axis=1)
            cq = cum[:, past:]
    y = _rmsnorm(x, p['final_norm_g']).astype(x.dtype)
    return y, k_new, v_new, lf_new, jnp.stack(rg_conv_new), jnp.stack(rg_h_new), jnp.stack(ffn_new)


def setup_inputs(seed: int = 0) -> dict:
    key = jax.random.key(seed)
    ks = jax.random.split(key, 40)

    def nrm(k, shape, scale):
        return jax.random.normal(k, shape, jnp.float32) * scale

    n_pages = PAST_LEN // PAGE_SIZE
    n_used = DEC_BATCH * n_pages
    n_phys = n_used + max(1, n_used // 4)
    u = jax.random.uniform(ks[22], (N_A_LAYERS, D_RNN), jnp.float32, 0.9, 0.999)
    a_base = u ** (1.0 / RG_C)
    return {
        'x_prompt': nrm(ks[0], (BATCH, SEQ, D_MODEL), 1.0),
        'x_sample': nrm(ks[1], (DEC_BATCH, DEC_SEQ, D_MODEL), 1.0),
        'c_prompt': nrm(ks[2], (BATCH, D_MODEL), 1.0),
        'c_sample': nrm(ks[3], (DEC_BATCH, D_MODEL), 1.0),
        'cache_k': nrm(ks[4], (n_phys, PAGE_SIZE, N_HEADS, HEAD_DIM), 1.0),
        'cache_v': nrm(ks[5], (n_phys, PAGE_SIZE, N_HEADS, HEAD_DIM), 1.0),
        'cache_logf': jax.nn.log_sigmoid(nrm(ks[6], (n_phys, PAGE_SIZE, N_HEADS), 1.0) + 3.0),
        'page_table': jax.random.permutation(ks[7], n_phys)[:n_used].reshape(DEC_BATCH, n_pages).astype(jnp.int32),
        'state_rglru_conv': nrm(ks[8], (N_A_LAYERS, DEC_BATCH, RG_CONV_W - 1, D_RNN), 1.0),
        'state_rglru_h': nrm(ks[9], (N_A_LAYERS, DEC_BATCH, D_RNN), 0.5),
        'state_ffn_conv': nrm(ks[10], (DEPTH, DEC_BATCH, FFN_CONV_W - 1, 2 * D_FF), 1.0),
        'mod_w': nrm(ks[11], (DEPTH, D_MODEL, 6 * D_MODEL), 0.1 * D_MODEL ** -0.5),
        'mod_b': nrm(ks[12], (DEPTH, 6 * D_MODEL), 0.02),
        'norm_mix_g': 1.0 + nrm(ks[13], (DEPTH, D_MODEL), 0.05),
        'norm_ffn_g': 1.0 + nrm(ks[14], (DEPTH, D_MODEL), 0.05),
        'rg_w_in': nrm(ks[15], (N_A_LAYERS, D_MODEL, 2 * D_RNN), D_MODEL ** -0.5),
        'rg_conv_w': nrm(ks[16], (N_A_LAYERS, RG_CONV_W, D_RNN), RG_CONV_W ** -0.5),
        'rg_conv_b': nrm(ks[17], (N_A_LAYERS, D_RNN), 0.02),
        'rg_wa': nrm(ks[18], (N_A_LAYERS, RG_BLOCKS, RG_BLOCK_W, RG_BLOCK_W), RG_BLOCK_W ** -0.5),
        'rg_ba': nrm(ks[19], (N_A_LAYERS, D_RNN), 0.02),
        'rg_wx': nrm(ks[20], (N_A_LAYERS, RG_BLOCKS, RG_BLOCK_W, RG_BLOCK_W), RG_BLOCK_W ** -0.5),
        'rg_bx': nrm(ks[21], (N_A_LAYERS, D_RNN), 0.02),
        'rg_lambda': jnp.log(a_base) - jnp.log1p(-a_base),
        'rg_w_out': nrm(ks[23], (N_A_LAYERS, D_RNN, D_MODEL), D_RNN ** -0.5),
        'kv_norm_g': 1.0 + nrm(ks[24], (D_MODEL,), 0.05),
        'kv_mod_w': nrm(ks[25], (D_MODEL, 2 * D_MODEL), 0.1 * D_MODEL ** -0.5),
        'kv_mod_b': nrm(ks[26], (2 * D_MODEL,), 0.02),
        'kv_w': nrm(ks[27], (D_MODEL, 2 * N_HEADS * HEAD_DIM + N_HEADS), D_MODEL ** -0.5),
        'kv_b_f': jax.random.uniform(ks[28], (N_HEADS,), jnp.float32, 1.0, 5.0),
        'fa_wq': nrm(ks[29], (N_B_LAYERS, D_MODEL, N_HEADS * HEAD_DIM), D_MODEL ** -0.5),
        'fa_wo': nrm(ks[30], (N_B_LAYERS, N_HEADS * HEAD_DIM, D_MODEL), (N_HEADS * HEAD_DIM) ** -0.5),
        'ffn_w_up': nrm(ks[31], (DEPTH, D_MODEL, 2 * D_FF), D_MODEL ** -0.5),
        'ffn_conv_w': nrm(ks[32], (DEPTH, FFN_CONV_W, 2 * D_FF), FFN_CONV_W ** -0.5),
        'ffn_conv_b': nrm(ks[33], (DEPTH, 2 * D_FF), 0.02),
        'ffn_w_down': nrm(ks[34], (DEPTH, D_FF, D_MODEL), D_FF ** -0.5),
        'final_norm_g': 1.0 + nrm(ks[35], (D_MODEL,), 0.05),
    }


def reference(x_prompt, x_sample, c_prompt, c_sample, cache_k, cache_v, cache_logf, page_table,
              state_rglru_conv, state_rglru_h, state_ffn_conv, mod_w, mod_b, norm_mix_g, norm_ffn_g,
              rg_w_in, rg_conv_w, rg_conv_b, rg_wa, rg_ba, rg_wx, rg_bx, rg_lambda, rg_w_out,
              kv_norm_g, kv_mod_w, kv_mod_b, kv_w, kv_b_f, fa_wq, fa_wo,
              ffn_w_up, ffn_conv_w, ffn_conv_b, ffn_w_down, final_norm_g):
    p = dict(mod_w=mod_w, mod_b=mod_b, norm_mix_g=norm_mix_g, norm_ffn_g=norm_ffn_g,
             rg_w_in=rg_w_in, rg_conv_w=rg_conv_w, rg_conv_b=rg_conv_b, rg_wa=rg_wa, rg_ba=rg_ba,
             rg_wx=rg_wx, rg_bx=rg_bx, rg_lambda=rg_lambda, rg_w_out=rg_w_out,
             kv_norm_g=kv_norm_g, kv_mod_w=kv_mod_w, kv_mod_b=kv_mod_b, kv_w=kv_w, kv_b_f=kv_b_f,
             fa_wq=fa_wq, fa_wo=fa_wo, ffn_w_up=ffn_w_up, ffn_conv_w=ffn_conv_w,
             ffn_conv_b=ffn_conv_b, ffn_w_down=ffn_w_down, final_norm_g=final_norm_g)

    bsz = x_prompt.shape[0]
    empty_kv = jnp.zeros((bsz, 0, N_HEADS, HEAD_DIM), x_prompt.dtype)
    empty_lf = jnp.zeros((bsz, 0, N_HEADS), jnp.float32)
    zero_rc = jnp.zeros((N_A_LAYERS, bsz, RG_CONV_W - 1, D_RNN), x_prompt.dtype)
    zero_rh = jnp.zeros((N_A_LAYERS, bsz, D_RNN), jnp.float32)
    zero_fc = jnp.zeros((DEPTH, bsz, FFN_CONV_W - 1, 2 * D_FF), x_prompt.dtype)
    y_prompt, p_k, p_v, p_lf, p_rc, p_rh, p_fc = _trunk(
        x_prompt, c_prompt, empty_kv, empty_kv, empty_lf, zero_rc, zero_rh, zero_fc, p)

    dbsz = x_sample.shape[0]
    past_len = page_table.shape[1] * cache_k.shape[1]
    k_past = cache_k[page_table].reshape(dbsz, past_len, N_HEADS, HEAD_DIM)
    v_past = cache_v[page_table].reshape(dbsz, past_len, N_HEADS, HEAD_DIM)
    lf_past = cache_logf[page_table].reshape(dbsz, past_len, N_HEADS)
    y_sample, s_k, s_v, s_lf, s_rc, s_rh, s_fc = _trunk(
        x_sample, c_sample, k_past, v_past, lf_past, state_rglru_conv, state_rglru_h, state_ffn_conv, p)

    return (y_prompt, y_sample, p_k, p_v, p_lf, p_rc, p_rh, p_fc, s_k, s_v, s_lf, s_rc, s_rh, s_fc)
```

```python
import functools

import jax
import jax.numpy as jnp
from jax import lax
from jax.experimental import pallas as pl
from jax.experimental.pallas import tpu as pltpu

D_MODEL = 1024
DEPTH = 4
N_A_LAYERS = 2
D_RNN = 1280
RG_BLOCKS = 16
RG_BLOCK_W = D_RNN // RG_BLOCKS
RG_HALF = D_RNN // 2
RG_C = 8.0
N_HEADS = 16
HEAD_DIM = 64
HW = N_HEADS * HEAD_DIM
D_FF = 3072
EPS = 1e-6
PAGE_SIZE = 128
LANES = 128
SUBLANES = 8
FFN_CHUNK = 512
VMEM_LIMIT = 56 * 1024 * 1024

F32 = jnp.float32
BF16 = jnp.bfloat16


def _rmsnorm(x, g):
    ms = jnp.mean(x * x, axis=-1, keepdims=True)
    return x * lax.rsqrt(ms + EPS) * g


def _modnorm(x, g, shift, scale):
    return _rmsnorm(x, g) * (1.0 + scale) + shift


def _softplus(z):
    return jnp.maximum(z, 0.0) + jnp.log1p(jnp.exp(-jnp.abs(z)))


def _log_sigmoid(z):
    return -_softplus(-z)


def _dot(a, b):
    return jnp.dot(a, b, preferred_element_type=F32)


def _causal_conv(x3, prev8, w, b):
    width = w.shape[0]
    tt = x3.shape[1]
    t8 = lax.broadcasted_iota(jnp.int32, (1, SUBLANES, x3.shape[2]), 1)
    y = None
    for k in range(width):
        s = width - 1 - k
        if s == 0:
            xs = x3
        else:
            rolled = pltpu.roll(x3, s, axis=1)
            first = jnp.where(t8 < s, pltpu.roll(prev8, s, axis=1), rolled[:, :SUBLANES])
            xs = first if tt == SUBLANES else jnp.concatenate([first, rolled[:, SUBLANES:]], axis=1)
        term = xs * w[k:k + 1][None]
        y = term if y is None else y + term
    return y + b[None]


def _tile_scan(a3, b3):
    t8 = lax.broadcasted_iota(jnp.int32, (1,) + a3.shape[1:], 1) % SUBLANES
    for s in (1, 2, 4):
        keep = t8 >= s
        a_sh = jnp.where(keep, pltpu.roll(a3, s, axis=1), 1.0)
        b_sh = jnp.where(keep, pltpu.roll(b3, s, axis=1), 0.0)
        b3 = a3 * b_sh + b3
        a3 = a3 * a_sh
    return a3, b3


def _mod_kernel(c_ref, w_ref, b_ref, o_ref):
    c = c_ref[...]
    act = c * jax.nn.sigmoid(c)
    o_ref[0] = _dot(act.astype(BF16), w_ref[0].astype(BF16)) + b_ref[0]


def _rglru_kernel(x_ref, g_ref, sh_ref, sc_ref, gt_ref, hist0_ref, h0_ref, win_ref, cw_ref, cb_ref,
                  wa_ref, wx_ref, ba_ref, bx_ref, lam_ref, wout_ref,
                  xo_ref, tail_ref, hlast_ref, a_s, b_s, hs_s):
    bb, tt, d = x_ref.shape
    c = D_RNN
    r = bb * tt

    @pl.when(pl.program_id(1) == 0)
    def _():
        tail_ref[...] = hist0_ref[...]
        hlast_ref[...] = h0_ref[...]

    x = x_ref[...]
    hn = _modnorm(x, g_ref[...][None], sh_ref[...], sc_ref[...])
    u = _dot(hn.reshape(r, d).astype(BF16), win_ref[...])
    gate = u[:, :c]
    xb3 = u[:, c:].reshape(bb, tt, c)
    xc3 = _causal_conv(xb3, tail_ref[...], cw_ref[...], cb_ref[...])
    tail_ref[...] = xb3[:, tt - SUBLANES:, :]
    xc = xc3.reshape(r, c)
    xcb = xc.astype(BF16)
    lo, hi = xcb[:, :RG_HALF], xcb[:, RG_HALF:]
    ra = jnp.concatenate([_dot(lo, wa_ref[0]), _dot(hi, wa_ref[1])], axis=1) + ba_ref[...]
    ia = jnp.concatenate([_dot(lo, wx_ref[0]), _dot(hi, wx_ref[1])], axis=1) + bx_ref[...]
    rg = jax.nn.sigmoid(ra)
    ig = jax.nn.sigmoid(ia)
    log_a = (-RG_C * rg) * _softplus(-lam_ref[...])
    a = jnp.exp(log_a)
    bv = jnp.sqrt(1.0 - a * a) * (ig * xc)
    a3, b3 = _tile_scan(a.reshape(bb, tt, c), bv.reshape(bb, tt, c))
    if tt == SUBLANES:
        hs3 = a3 * hlast_ref[...] + b3
        hlast_ref[...] = hs3[:, tt - 1:tt, :]
        hs = hs3.reshape(r, c)
    else:
        a_s[...] = a3.reshape(r, c)
        b_s[...] = b3.reshape(r, c)

        def tile_step(j, h):
            r0 = pl.multiple_of(j * SUBLANES, SUBLANES)
            hs_t = a_s[pl.ds(r0, SUBLANES), :] * h + b_s[pl.ds(r0, SUBLANES), :]
            hs_s[pl.ds(r0, SUBLANES), :] = hs_t
            return hs_t[SUBLANES - 1:SUBLANES, :]

        hlast_ref[0] = lax.fori_loop(0, r // SUBLANES, tile_step, hlast_ref[0], unroll=4)
        hs = hs_s[...]
    z = (jax.nn.gelu(gate) * hs).astype(BF16)
    y = _dot(z, wout_ref[...]).reshape(bb, tt, d)
    xo_ref[...] = x + (1.0 + gt_ref[...]) * y


def _ffn_kernel(*refs, final):
    if final:
        (x_ref, g_ref, sh_ref, sc_ref, gt_ref, hist0_ref, wup_ref, cw_ref, cb_ref, wdn_ref, gf_ref,
         xo_ref, tail_ref) = refs
    else:
        (x_ref, g_ref, sh_ref, sc_ref, gt_ref, hist0_ref, wup_ref, cw_ref, cb_ref, wdn_ref,
         xo_ref, tail_ref) = refs
    bb, tt, d = x_ref.shape
    r = bb * tt

    @pl.when(pl.program_id(1) == 0)
    def _():
        tail_ref[...] = hist0_ref[...]

    x = x_ref[...]
    hn = _modnorm(x, g_ref[...][None], sh_ref[...], sc_ref[...]).reshape(r, d).astype(BF16)
    acc = jnp.zeros((r, d), F32)
    for n in range(D_FF // FFN_CHUNK):
        parts = []
        for lo in (n * FFN_CHUNK, D_FF + n * FFN_CHUNK):
            cols = slice(lo, lo + FFN_CHUNK)
            u3 = _dot(hn, wup_ref[:, cols]).reshape(bb, tt, FFN_CHUNK)
            parts.append(_causal_conv(u3, tail_ref[:, :, cols], cw_ref[:, cols], cb_ref[:, cols]))
            tail_ref[:, :, cols] = u3[:, tt - SUBLANES:, :]
        z = (jax.nn.gelu(parts[0]) * parts[1]).reshape(r, FFN_CHUNK).astype(BF16)
        acc = acc + _dot(z, wdn_ref[n * FFN_CHUNK:(n + 1) * FFN_CHUNK, :])
    xn = x + (1.0 + gt_ref[...]) * acc.reshape(bb, tt, d)
    if final:
        xn = _rmsnorm(xn, gf_ref[...][None])
    xo_ref[...] = xn


def _kv_kernel(*refs, prompt):
    if prompt:
        (x_ref, g_ref, sh_ref, sc_ref, wk_ref, wv_ref, wf_ref, bf_ref, cum0_ref,
         k_ref, v_ref, lf_ref, cum_ref, kt_ref, vb_ref, ckt_ref, carry) = refs
    else:
        (x_ref, g_ref, sh_ref, sc_ref, wk_ref, wv_ref, wf_ref, bf_ref, cum0_ref,
         k_ref, v_ref, lf_ref, cum_ref, carry) = refs
    bb, tt, d = x_ref.shape
    r = bb * tt

    @pl.when(pl.program_id(1) == 0)
    def _():
        carry[...] = cum0_ref[...]

    hk = _modnorm(x_ref[...], g_ref[...][None], sh_ref[...], sc_ref[...]).reshape(r, d).astype(BF16)
    k = _dot(hk, wk_ref[...])
    v = _dot(hk, wv_ref[...])
    lf3 = _log_sigmoid(_dot(hk, wf_ref[...]) + bf_ref[...]).reshape(bb, tt, LANES)
    k_ref[...] = k.reshape(bb, tt, HW)
    v_ref[...] = v.reshape(bb, tt, HW)
    lf_ref[...] = lf3
    tpos = lax.broadcasted_iota(jnp.int32, (1, tt, LANES), 1)
    cs = lf3
    s = 1
    while s < tt:
        cs = cs + jnp.where(tpos >= s, pltpu.roll(cs, s, axis=1), 0.0)
        s *= 2
    cum = cs + carry[...]
    carry[...] = cum[:, tt - 1:tt, :]
    cum_ref[...] = cum
    if prompt:
        kt_ref[0] = k.T.astype(BF16)
        vb_ref[0] = v.astype(BF16)
        ckt_ref[0] = cum[0].T[:N_HEADS, :]


def _qproj_kernel(x_ref, g_ref, sh_ref, sc_ref, wq_ref, q_ref):
    bb, tt, d = x_ref.shape
    hn = _modnorm(x_ref[...], g_ref[...][None], sh_ref[...], sc_ref[...])
    q = _dot(hn.reshape(bb * tt, d).astype(BF16), wq_ref[...]) * (HEAD_DIM ** -0.5)
    q_ref[...] = q.reshape(bb, tt, HW).astype(q_ref.dtype)


def _oproj_kernel(x_ref, o_ref, gt_ref, wo_ref, xo_ref):
    bb, tt, d = x_ref.shape
    y = _dot(o_ref[...].reshape(bb * tt, HW).astype(BF16), wo_ref[...]).reshape(bb, tt, d)
    xo_ref[...] = x_ref[...] + (1.0 + gt_ref[...]) * y


def _flash_kernel(q_ref, kt_ref, v_ref, cq_ref, ck_ref, o_ref, m_s, l_s, acc_s, cq_s):
    pair = pl.program_id(1)
    qi = pl.program_id(2)
    ki = pl.program_id(3)
    tq = q_ref.shape[1]
    tk = v_ref.shape[1]
    lane = lax.broadcasted_iota(jnp.int32, (1, LANES), 1)

    @pl.when(ki == 0)
    def _():
        m_s[...] = jnp.full(m_s.shape, -jnp.inf, F32)
        l_s[...] = jnp.zeros(l_s.shape, F32)
        acc_s[...] = jnp.zeros(acc_s.shape, F32)
        cqb = cq_ref[0]
        for j in range(2):
            cq_s[j] = jnp.sum(jnp.where(lane == 2 * pair + j, cqb, 0.0), axis=1, keepdims=True)

    def step(masked):
        q2 = q_ref[0]
        kt = kt_ref[0]
        vb = v_ref[0]
        for j in range(2):
            in_head = (lane >= j * HEAD_DIM) & (lane < (j + 1) * HEAD_DIM)
            qm = jnp.where(in_head, q2, jnp.zeros_like(q2))
            s = _dot(qm, kt)
            s = s + cq_s[j] - ck_ref[0, pl.ds(2 * pair + j, 1), :]
            if masked:
                qpos = lax.broadcasted_iota(jnp.int32, (tq, tk), 0)
                kpos = lax.broadcasted_iota(jnp.int32, (tq, tk), 1)
                s = jnp.where(kpos <= qpos, s, -jnp.inf)
            m_prev = m_s[j]
            m_new = jnp.maximum(m_prev, jnp.max(s, axis=1, keepdims=True))
            alpha = jnp.exp(m_prev - m_new)
            p = jnp.exp(s - m_new)
            l_s[j] = alpha * l_s[j] + jnp.sum(p, axis=1, keepdims=True)
            acc_s[j] = alpha * acc_s[j] + _dot(p.astype(BF16), vb)
            m_s[j] = m_new

    @pl.when(ki < qi)
    def _():
        step(False)

    @pl.when(ki == qi)
    def _():
        step(True)
        o0 = acc_s[0] / l_s[0]
        o1 = acc_s[1] / l_s[1]
        o_ref[0] = jnp.where(lane < HEAD_DIM, o0, o1).astype(o_ref.dtype)


def _paged_cum_kernel(pt_ref, lft_ref, ck_ref, tot_ref, carry):
    @pl.when(pl.program_id(1) == 0)
    def _():
        carry[...] = jnp.zeros(carry.shape, F32)

    lane = lax.broadcasted_iota(jnp.int32, (N_HEADS, PAGE_SIZE), 1)
    cs = lft_ref[0]
    s = 1
    while s < PAGE_SIZE:
        cs = cs + jnp.where(lane >= s, pltpu.roll(cs, s, axis=1), 0.0)
        s *= 2
    cum = cs + carry[...]
    ck_ref[0] = cum
    total = jnp.broadcast_to(cum[:, PAGE_SIZE - 1:PAGE_SIZE], carry.shape)
    carry[...] = total
    tot_ref[0] = total


def _decode_attn_kernel(pt_ref, q_ref, kp_ref, vp_ref, ck_ref, cq_ref, kn_ref, vn_ref, ckn_ref,
                        o_ref, qbd_s, m_s, l_s, acc_s):
    j = pl.program_id(1)
    n_pages = pl.num_programs(1) - 1
    rows = qbd_s.shape[0]
    nt = rows // N_HEADS
    head_mask = (lax.broadcasted_iota(jnp.int32, (N_HEADS, HW), 1) // HEAD_DIM
                 == lax.broadcasted_iota(jnp.int32, (N_HEADS, HW), 0))

    @pl.when(j == 0)
    def _():
        m_s[...] = jnp.full(m_s.shape, -jnp.inf, F32)
        l_s[...] = jnp.zeros(l_s.shape, F32)
        acc_s[...] = jnp.zeros(acc_s.shape, F32)
        q = q_ref[0]
        qbd = jnp.where(head_mask[None], q[:, None, :], 0.0)
        qbd_s[...] = qbd.reshape(rows, HW).astype(BF16)

    def step(kb, vb, bias):
        s = lax.dot_general(qbd_s[...], kb, (((1,), (1,)), ((), ())), preferred_element_type=F32)
        s = s + bias
        m_prev = m_s[...]
        m_new = jnp.maximum(m_prev, jnp.max(s, axis=1, keepdims=True))
        alpha = jnp.exp(m_prev - m_new)
        p = jnp.exp(s - m_new)
        l_s[...] = alpha * l_s[...] + jnp.sum(p, axis=1, keepdims=True)
        acc_s[...] = alpha * acc_s[...] + _dot(p.astype(BF16), vb)
        m_s[...] = m_new

    @pl.when(j < n_pages)
    def _():
        ck = ck_ref[0]
        bias = (cq_ref[0].reshape(nt, N_HEADS, 1) - ck[None]).reshape(rows, PAGE_SIZE)
        step(kp_ref[0], vp_ref[0], bias)

    @pl.when(j == n_pages)
    def _():
        pad = jnp.zeros((PAGE_SIZE - nt, HW), F32)
        kb = jnp.concatenate([kn_ref[0], pad], axis=0).astype(BF16)
        vb = jnp.concatenate([vn_ref[0], pad], axis=0).astype(BF16)
        tok = lax.broadcasted_iota(jnp.int32, (rows, PAGE_SIZE), 0) // N_HEADS
        key = lax.broadcasted_iota(jnp.int32, (rows, PAGE_SIZE), 1)
        bias = jnp.where(key <= tok, cq_ref[0] - ckn_ref[0], -jnp.inf)
        step(kb, vb, bias)
        o_full = (acc_s[...] / l_s[...]).reshape(nt, N_HEADS, HW)
        o_ref[0] = jnp.sum(jnp.where(head_mask[None], o_full, 0.0), axis=1)


def _resident(shape):
    nd = len(shape)
    return pl.BlockSpec(shape, lambda *_: (0,) * nd, pipeline_mode=pl.Buffered(1))


def _tok_spec(bb, tt, width):
    return pl.BlockSpec((bb, tt, width), lambda i, t: (i, t, 0))


def _row_spec(bb, rows, width):
    return pl.BlockSpec((bb, rows, width), lambda i, t: (i, 0, 0))


def _params(sem):
    return pltpu.CompilerParams(dimension_semantics=sem, vmem_limit_bytes=VMEM_LIMIT)


def _mod_call(c_all, w, b):
    n_layers, d, n = w.shape
    tn = 1536 if n % 1536 == 0 else 1024
    rows = c_all.shape[0]
    return pl.pallas_call(
        _mod_kernel,
        grid=(n_layers, n // tn),
        in_specs=[pl.BlockSpec((rows, d), lambda l, j: (0, 0)),
                  pl.BlockSpec((1, d, tn), lambda l, j: (l, 0, j)),
                  pl.BlockSpec((1, 1, tn), lambda l, j: (l, 0, j))],
        out_specs=pl.BlockSpec((1, rows, tn), lambda l, j: (l, 0, j)),
        out_shape=jax.ShapeDtypeStruct((n_layers, rows, n), F32),
        compiler_params=_params(("arbitrary", "arbitrary")),
        name="mod_proj",
    )(c_all, w, b.reshape(n_layers, 1, n))


def _rglru_call(x, mods, hist0, h0, w, bb, tt):
    b, t, d = x.shape
    c = D_RNN
    r = bb * tt
    sh, sc, gt = mods
    scratch_rows = r if tt != SUBLANES else SUBLANES
    return pl.pallas_call(
        _rglru_kernel,
        grid=(b // bb, t // tt),
        in_specs=[_tok_spec(bb, tt, d), _resident((1, d)),
                  _row_spec(bb, 1, d), _row_spec(bb, 1, d), _row_spec(bb, 1, d),
                  _row_spec(bb, SUBLANES, c), _row_spec(bb, 1, c),
                  _resident((d, 2 * c)), _resident(w["cw"].shape), _resident((1, c)),
                  _resident((2, RG_HALF, RG_HALF)), _resident((2, RG_HALF, RG_HALF)),
                  _resident((1, c)), _resident((1, c)), _resident((1, c)), _resident((c, d))],
        out_specs=[_tok_spec(bb, tt, d), _row_spec(bb, SUBLANES, c), _row_spec(bb, 1, c)],
        out_shape=[jax.ShapeDtypeStruct((b, t, d), F32),
                   jax.ShapeDtypeStruct((b, SUBLANES, c), F32),
                   jax.ShapeDtypeStruct((b, 1, c), F32)],
        scratch_shapes=[pltpu.VMEM((scratch_rows, c), F32)] * 3,
        compiler_params=_params(("arbitrary", "arbitrary")),
        name="rglru_mixer",
    )(x, w["g"], sh, sc, gt, hist0, h0, w["win"], w["cw"], w["cb"], w["wa"], w["wx"],
      w["ba"], w["bx"], w["lam"], w["wout"])


def _ffn_call(x, mods, hist0, w, bb, tt, final_g=None):
    b, t, d = x.shape
    sh, sc, gt = mods
    final = final_g is not None
    in_specs = [_tok_spec(bb, tt, d), _resident((1, d)),
                _row_spec(bb, 1, d), _row_spec(bb, 1, d), _row_spec(bb, 1, d),
                _row_spec(bb, SUBLANES, 2 * D_FF),
                _resident((d, 2 * D_FF)), _resident(w["cw"].shape), _resident((1, 2 * D_FF)),
                _resident((D_FF, d))]
    args = [x, w["g"], sh, sc, gt, hist0, w["wup"], w["cw"], w["cb"], w["wdn"]]
    if final:
        in_specs.append(_resident((1, d)))
        args.append(final_g)
    return pl.pallas_call(
        functools.partial(_ffn_kernel, final=final),
        grid=(b // bb, t // tt),
        in_specs=in_specs,
        out_specs=[_tok_spec(bb, tt, d), _row_spec(bb, SUBLANES, 2 * D_FF)],
        out_shape=[jax.ShapeDtypeStruct((b, t, d), F32),
                   jax.ShapeDtypeStruct((b, SUBLANES, 2 * D_FF), F32)],
        compiler_params=_params(("arbitrary", "arbitrary")),
        name="conv_ffn",
    )(*args)


def _kv_call(x, g, shift, scale, w, cum0, bb, tt, prompt):
    b, t, d = x.shape
    in_specs = [_tok_spec(bb, tt, d), _resident((1, d)), _row_spec(bb, 1, d), _row_spec(bb, 1, d),
                _resident((d, HW)), _resident((d, HW)), _resident((d, LANES)), _resident((1, LANES)),
                _row_spec(bb, 1, LANES)]
    out_specs = [_tok_spec(bb, tt, HW), _tok_spec(bb, tt, HW), _tok_spec(bb, tt, LANES),
                 _tok_spec(bb, tt, LANES)]
    out_shape = [jax.ShapeDtypeStruct((b, t, HW), F32), jax.ShapeDtypeStruct((b, t, HW), F32),
                 jax.ShapeDtypeStruct((b, t, LANES), F32), jax.ShapeDtypeStruct((b, t, LANES), F32)]
    if prompt:
        out_specs += [pl.BlockSpec((1, HW, tt), lambda i, j: (i, 0, j)), _tok_spec(1, tt, HW),
                      pl.BlockSpec((1, N_HEADS, tt), lambda i, j: (i, 0, j))]
        out_shape += [jax.ShapeDtypeStruct((b, HW, t), BF16), jax.ShapeDtypeStruct((b, t, HW), BF16),
                      jax.ShapeDtypeStruct((b, N_HEADS, t), F32)]
    return pl.pallas_call(
        functools.partial(_kv_kernel, prompt=prompt),
        grid=(b // bb, t // tt),
        in_specs=in_specs, out_specs=out_specs, out_shape=out_shape,
        scratch_shapes=[pltpu.VMEM((bb, 1, LANES), F32)],
        compiler_params=_params(("arbitrary", "arbitrary")),
        name="kv_proj",
    )(x, g, shift, scale, w["wk"], w["wv"], w["wf"], w["bf"], cum0)


def _qproj_call(x, g, mods, wq, bb, tt, dtype):
    b, t, d = x.shape
    sh, sc, _ = mods
    return pl.pallas_call(
        _qproj_kernel,
        grid=(b // bb, t // tt),
        in_specs=[_tok_spec(bb, tt, d), _resident((1, d)), _row_spec(bb, 1, d), _row_spec(bb, 1, d),
                  _resident((d, HW))],
        out_specs=_tok_spec(bb, tt, HW),
        out_shape=jax.ShapeDtypeStruct((b, t, HW), dtype),
        compiler_params=_params(("arbitrary", "arbitrary")),
        name="q_proj",
    )(x, g, sh, sc, wq)


def _oproj_call(x, o, gt, wo, bb, tt):
    b, t, d = x.shape
    return pl.pallas_call(
        _oproj_kernel,
        grid=(b // bb, t // tt),
        in_specs=[_tok_spec(bb, tt, d), _tok_spec(bb, tt, HW), _row_spec(bb, 1, d), _resident((HW, d))],
        out_specs=_tok_spec(bb, tt, d),
        out_shape=jax.ShapeDtypeStruct((b, t, d), F32),
        compiler_params=_params(("arbitrary", "arbitrary")),
        name="o_proj",
    )(x, o, gt, wo)


def _flash_call(q, kt, vb, cum, ckt, tq):
    b, t, _ = q.shape
    n_pairs = N_HEADS // 2
    nq = t // tq
    return pl.pallas_call(
        _flash_kernel,
        grid=(b, n_pairs, nq, nq),
        in_specs=[pl.BlockSpec((1, tq, LANES), lambda i, p, qi, ki: (i, qi, p)),
                  pl.BlockSpec((1, LANES, tq), lambda i, p, qi, ki: (i, p, jnp.minimum(ki, qi))),
                  pl.BlockSpec((1, tq, LANES), lambda i, p, qi, ki: (i, jnp.minimum(ki, qi), p)),
                  pl.BlockSpec((1, tq, LANES), lambda i, p, qi, ki: (i, qi, 0)),
                  pl.BlockSpec((1, N_HEADS, tq), lambda i, p, qi, ki: (i, 0, jnp.minimum(ki, qi)))],
        out_specs=pl.BlockSpec((1, tq, LANES), lambda i, p, qi, ki: (i, qi, p)),
        out_shape=jax.ShapeDtypeStruct((b, t, HW), BF16),
        scratch_shapes=[pltpu.VMEM((2, tq, 1), F32), pltpu.VMEM((2, tq, 1), F32),
                        pltpu.VMEM((2, tq, LANES), F32), pltpu.VMEM((2, tq, 1), F32)],
        compiler_params=_params(("arbitrary", "arbitrary", "arbitrary", "arbitrary")),
        name="fox_prefill",
    )(q, kt, vb, cum, ckt)


def _paged_cum_call(page_table, lft):
    b, n_pages = page_table.shape
    return pl.pallas_call(
        _paged_cum_kernel,
        grid_spec=pltpu.PrefetchScalarGridSpec(
            num_scalar_prefetch=1,
            grid=(b, n_pages),
            in_specs=[pl.BlockSpec((1, N_HEADS, PAGE_SIZE), lambda i, j, pt: (pt[i, j], 0, 0))],
            out_specs=[pl.BlockSpec((1, N_HEADS, PAGE_SIZE), lambda i, j, pt: (i, 0, j)),
                       pl.BlockSpec((1, N_HEADS, PAGE_SIZE), lambda i, j, pt: (i, 0, 0))],
            scratch_shapes=[pltpu.VMEM((N_HEADS, PAGE_SIZE), F32)]),
        out_shape=[jax.ShapeDtypeStruct((b, N_HEADS, n_pages * PAGE_SIZE), F32),
                   jax.ShapeDtypeStruct((b, N_HEADS, PAGE_SIZE), F32)],
        compiler_params=_params(("arbitrary", "arbitrary")),
        name="paged_logf_cumsum",
    )(page_table, lft)


def _decode_attn_call(page_table, q, kc, vc, ckt, cq_col, k_new, v_new, ck_new):
    b, nt, _ = q.shape
    n_pages = page_table.shape[1]
    rows = nt * N_HEADS
    last = n_pages - 1

    def page(i, j, pt):
        return (pt[i, jnp.minimum(j, last)], 0, 0)

    return pl.pallas_call(
        _decode_attn_kernel,
        grid_spec=pltpu.PrefetchScalarGridSpec(
            num_scalar_prefetch=1,
            grid=(b, n_pages + 1),
            in_specs=[pl.BlockSpec((1, nt, HW), lambda i, j, pt: (i, 0, 0)),
                      pl.BlockSpec((1, PAGE_SIZE, HW), page),
                      pl.BlockSpec((1, PAGE_SIZE, HW), page),
                      pl.BlockSpec((1, N_HEADS, PAGE_SIZE), lambda i, j, pt: (i, 0, jnp.minimum(j, last))),
                      pl.BlockSpec((1, rows, 1), lambda i, j, pt: (i, 0, 0)),
                      pl.BlockSpec((1, nt, HW), lambda i, j, pt: (i, 0, 0)),
                      pl.BlockSpec((1, nt, HW), lambda i, j, pt: (i, 0, 0)),
                      pl.BlockSpec((1, rows, PAGE_SIZE), lambda i, j, pt: (i, 0, 0))],
            out_specs=pl.BlockSpec((1, nt, HW), lambda i, j, pt: (i, 0, 0)),
            scratch_shapes=[pltpu.VMEM((rows, HW), BF16), pltpu.VMEM((rows, 1), F32),
                            pltpu.VMEM((rows, 1), F32), pltpu.VMEM((rows, HW), F32)]),
        out_shape=jax.ShapeDtypeStruct((b, nt, HW), F32),
        compiler_params=_params(("arbitrary", "arbitrary")),
        name="fox_decode",
    )(page_table, q, kc, vc, ckt, cq_col, k_new, v_new, ck_new)


def _block_diag_halves(w):
    per = RG_BLOCKS // 2
    eye = jnp.eye(per, dtype=w.dtype)
    w2 = w.reshape(2, per, RG_BLOCK_W, RG_BLOCK_W)
    dense = w2[:, :, :, None, :] * eye[None, :, None, :, None]
    return dense.reshape(2, RG_HALF, RG_HALF)


def _prep_weights(p):
    row = lambda a: a.reshape(1, -1)
    layers = []
    for l in range(DEPTH):
        w = {"mix_g": row(p["norm_mix_g"][l]),
             "ffn": {"g": row(p["norm_ffn_g"][l]), "wup": p["ffn_w_up"][l].astype(BF16),
                     "cw": p["ffn_conv_w"][l], "cb": row(p["ffn_conv_b"][l]),
                     "wdn": p["ffn_w_down"][l].astype(BF16)}}
        if l < N_A_LAYERS:
            w["rg"] = {"g": w["mix_g"], "win": p["rg_w_in"][l].astype(BF16), "cw": p["rg_conv_w"][l],
                       "cb": row(p["rg_conv_b"][l]),
                       "wa": _block_diag_halves(p["rg_wa"][l]).astype(BF16),
                       "wx": _block_diag_halves(p["rg_wx"][l]).astype(BF16),
                       "ba": row(p["rg_ba"][l]), "bx": row(p["rg_bx"][l]), "lam": row(p["rg_lambda"][l]),
                       "wout": p["rg_w_out"][l].astype(BF16)}
        else:
            j = l - N_A_LAYERS
            w["wq"] = p["fa_wq"][j].astype(BF16)
            w["wo"] = p["fa_wo"][j].astype(BF16)
        layers.append(w)
    kvw = p["kv_w"]
    kv = {"g": row(p["kv_norm_g"]), "wk": kvw[:, :HW].astype(BF16), "wv": kvw[:, HW:2 * HW].astype(BF16),
          "wf": jnp.pad(kvw[:, 2 * HW:], ((0, 0), (0, LANES - N_HEADS))).astype(BF16),
          "bf": jnp.pad(p["kv_b_f"], (0, LANES - N_HEADS)).reshape(1, LANES)}
    return layers, kv


def _trunk(x, mod, kv_mod, rg_conv_buf, rg_h0, ffn_buf, layers, kvw, final_g, bb, tt, past):
    bsz, t, d = x.shape
    prompt = past is None
    lead = SUBLANES - rg_conv_buf.shape[2]
    rg_hist = jnp.pad(rg_conv_buf, ((0, 0), (0, 0), (lead, 0), (0, 0)))
    lead_f = SUBLANES - ffn_buf.shape[2]
    ffn_hist = jnp.pad(ffn_buf, ((0, 0), (0, 0), (lead_f, 0), (0, 0)))
    rg_conv_new, rg_h_new, ffn_new = [], [], []
    for l in range(DEPTH):
        w = layers[l]
        m6 = [mod[l][:, i * d:(i + 1) * d].reshape(bsz, 1, d) for i in range(6)]
        mix_mods, ffn_mods = m6[:3], m6[3:]
        if l < N_A_LAYERS:
            x, tail, hl = _rglru_call(x, mix_mods, rg_hist[l], rg_h0[l].reshape(bsz, 1, D_RNN),
                                      w["rg"], bb, tt)
            rg_conv_new.append(tail[:, lead:, :])
            rg_h_new.append(hl.reshape(bsz, D_RNN))
        else:
            if prompt:
                q = _qproj_call(x, w["mix_g"], mix_mods, w["wq"], bb, tt, BF16)
                o = _flash_call(q, kt, v_bf, cum, ckt, 512)
            else:
                q = _qproj_call(x, w["mix_g"], mix_mods, w["wq"], bb, tt, F32)
                o = _decode_attn_call(past["page_table"], q, past["kc"], past["vc"], past["ckt"],
                                      cq_col, k_new, v_new, ck_new_rows)
            x = _oproj_call(x, o, mix_mods[2], w["wo"], bb, tt)
        fin = final_g if l == DEPTH - 1 else None
        x, ftail = _ffn_call(x, ffn_mods, ffn_hist[l], w["ffn"], bb, tt, fin)
        ffn_new.append(ftail[:, lead_f:, :])
        if l == N_A_LAYERS - 1:
            kv_shift = kv_mod[:, :d].reshape(bsz, 1, d)
            kv_scale = kv_mod[:, d:].reshape(bsz, 1, d)
            if prompt:
                cum0 = jnp.zeros((bsz, 1, LANES), F32)
                k_new, v_new, lf, cum, kt, v_bf, ckt = _kv_call(
                    x, kvw["g"], kv_shift, kv_scale, kvw, cum0, bb, tt, True)
            else:
                cum0 = jnp.pad(past["total"], ((0, 0), (0, LANES - N_HEADS))).reshape(bsz, 1, LANES)
                k_new, v_new, lf, cum = _kv_call(x, kvw["g"], kv_shift, kv_scale, kvw, cum0, bb, tt, False)
                cum_new = cum[:, :, :N_HEADS]
                cq_col = cum_new.reshape(bsz, t * N_HEADS, 1)
                ck_t = jnp.transpose(cum_new, (0, 2, 1))
                ck_t = jnp.pad(ck_t, ((0, 0), (0, 0), (0, PAGE_SIZE - t)))
                ck_new_rows = jnp.tile(ck_t, (1, t, 1))
    k4 = k_new.reshape(bsz, t, N_HEADS, HEAD_DIM)
    v4 = v_new.reshape(bsz, t, N_HEADS, HEAD_DIM)
    return (x, k4, v4, lf[:, :, :N_HEADS], jnp.stack(rg_conv_new), jnp.stack(rg_h_new), jnp.stack(ffn_new))


def kernel(x_prompt, x_sample, c_prompt, c_sample, cache_k, cache_v, cache_logf, page_table,
           state_rglru_conv, state_rglru_h, state_ffn_conv, mod_w, mod_b, norm_mix_g, norm_ffn_g,
           rg_w_in, rg_conv_w, rg_conv_b, rg_wa, rg_ba, rg_wx, rg_bx, rg_lambda, rg_w_out,
           kv_norm_g, kv_mod_w, kv_mod_b, kv_w, kv_b_f, fa_wq, fa_wo,
           ffn_w_up, ffn_conv_w, ffn_conv_b, ffn_w_down, final_norm_g):
    p = dict(norm_mix_g=norm_mix_g, norm_ffn_g=norm_ffn_g, rg_w_in=rg_w_in, rg_conv_w=rg_conv_w,
             rg_conv_b=rg_conv_b, rg_wa=rg_wa, rg_ba=rg_ba, rg_wx=rg_wx, rg_bx=rg_bx,
             rg_lambda=rg_lambda, rg_w_out=rg_w_out, kv_norm_g=kv_norm_g, kv_w=kv_w, kv_b_f=kv_b_f,
             fa_wq=fa_wq, fa_wo=fa_wo, ffn_w_up=ffn_w_up, ffn_conv_w=ffn_conv_w,
             ffn_conv_b=ffn_conv_b, ffn_w_down=ffn_w_down)
    layers, kvw = _prep_weights(p)
    final_g = final_norm_g.reshape(1, D_MODEL)

    nb_p, nb_s = x_prompt.shape[0], x_sample.shape[0]
    n_all = nb_p + nb_s
    rows = -(-n_all // SUBLANES) * SUBLANES
    c_all = jnp.pad(jnp.concatenate([c_prompt, c_sample], axis=0), ((0, rows - n_all), (0, 0)))
    mod_all = _mod_call(c_all, mod_w, mod_b)
    kv_mod_all = _mod_call(c_all, kv_mod_w[None], kv_mod_b[None])[0]

    zero_rc = jnp.zeros((N_A_LAYERS, nb_p) + state_rglru_conv.shape[2:], F32)
    zero_rh = jnp.zeros((N_A_LAYERS, nb_p, D_RNN), F32)
    zero_fc = jnp.zeros((DEPTH, nb_p) + state_ffn_conv.shape[2:], F32)
    out_p = _trunk(x_prompt, mod_all[:, :nb_p], kv_mod_all[:nb_p], zero_rc, zero_rh, zero_fc,
                   layers, kvw, final_g, 1, 256, None)

    n_phys = cache_k.shape[0]
    kc = cache_k.reshape(n_phys, PAGE_SIZE, HW).astype(BF16)
    vc = cache_v.reshape(n_phys, PAGE_SIZE, HW).astype(BF16)
    lft = jnp.transpose(cache_logf, (0, 2, 1))
    ckt, tot = _paged_cum_call(page_table, lft)
    past = {"page_table": page_table, "kc": kc, "vc": vc, "ckt": ckt, "total": tot[:, :, 0]}
    out_s = _trunk(x_sample, mod_all[:, nb_p:n_all], kv_mod_all[nb_p:n_all], state_rglru_conv,
                   state_rglru_h, state_ffn_conv, layers, kvw, final_g, 32, SUBLANES, past)
    return (out_p[0], out_s[0]) + out_p[1:] + out_s[1:]
```

```python
import functools

import jax
import jax.numpy as jnp
from jax import lax
from jax.experimental import pallas as pl
from jax.experimental.pallas import tpu as pltpu

D_MODEL = 1024
DEPTH = 4
N_A_LAYERS = 2
D_RNN = 1280
RG_BLOCKS = 16
RG_BLOCK_W = D_RNN // RG_BLOCKS
RG_HALF = D_RNN // 2
RG_C = 8.0
N_HEADS = 16
HEAD_DIM = 64
HW = N_HEADS * HEAD_DIM
D_FF = 3072
EPS = 1e-6
PAGE_SIZE = 128
LANES = 128
SUBLANES = 8
FFN_CHUNK = 512
N_SPLIT = 3
FLASH_BLOCK = 512
FLASH_HEADS = 4
DECODE_PAGES = 8
VMEM_LIMIT = 56 * 1024 * 1024

F32 = jnp.float32
BF16 = jnp.bfloat16


def _rmsnorm(x, g):
    ms = jnp.mean(x * x, axis=-1, keepdims=True)
    return x * lax.rsqrt(ms + EPS) * g


def _modnorm(x, g, shift, scale):
    return _rmsnorm(x, g) * (1.0 + scale) + shift


def _softplus(z):
    return jnp.maximum(z, 0.0) + jnp.log1p(jnp.exp(-jnp.abs(z)))


def _log_sigmoid(z):
    return -_softplus(-z)


def _dot(a, b):
    return jnp.dot(a, b, preferred_element_type=F32)


def _causal_conv(x3, prev8, w, b):
    width = w.shape[0]
    tt = x3.shape[1]
    t8 = lax.broadcasted_iota(jnp.int32, (1, SUBLANES, x3.shape[2]), 1)
    y = None
    for k in range(width):
        s = width - 1 - k
        if s == 0:
            xs = x3
        else:
            rolled = pltpu.roll(x3, s, axis=1)
            first = jnp.where(t8 < s, pltpu.roll(prev8, s, axis=1), rolled[:, :SUBLANES])
            xs = first if tt == SUBLANES else jnp.concatenate([first, rolled[:, SUBLANES:]], axis=1)
        term = xs * w[k:k + 1][None]
        y = term if y is None else y + term
    return y + b[None]


def _tile_scan(a3, b3):
    t8 = lax.broadcasted_iota(jnp.int32, (1,) + a3.shape[1:], 1) % SUBLANES
    for s in (1, 2, 4):
        keep = t8 >= s
        a_sh = jnp.where(keep, pltpu.roll(a3, s, axis=1), 1.0)
        b_sh = jnp.where(keep, pltpu.roll(b3, s, axis=1), 0.0)
        b3 = a3 * b_sh + b3
        a3 = a3 * a_sh
    return a3, b3


def _mod_kernel(c_ref, w_ref, b_ref, o_ref):
    c = c_ref[...]
    act = c * jax.nn.sigmoid(c)
    o_ref[0] = _dot(act.astype(BF16), w_ref[0].astype(BF16)) + b_ref[0]


def _rglru_kernel(x_ref, g_ref, sh_ref, sc_ref, gt_ref, hist0_ref, h0_ref, win_ref, cw_ref, cb_ref,
                  wa_ref, wx_ref, ba_ref, bx_ref, lam_ref, wout_ref,
                  xo_ref, tail_ref, hlast_ref, a_s, b_s, hs_s):
    bb, tt, d = x_ref.shape
    c = D_RNN
    r = bb * tt

    @pl.when(pl.program_id(1) == 0)
    def _():
        tail_ref[...] = hist0_ref[...]
        hlast_ref[...] = h0_ref[...]

    x = x_ref[...]
    hn = _modnorm(x, g_ref[...][None], sh_ref[...], sc_ref[...])
    u = _dot(hn.reshape(r, d).astype(BF16), win_ref[...])
    gate = u[:, :c]
    xb3 = u[:, c:].reshape(bb, tt, c)
    xc3 = _causal_conv(xb3, tail_ref[...], cw_ref[...], cb_ref[...])
    tail_ref[...] = xb3[:, tt - SUBLANES:, :]
    xc = xc3.reshape(r, c)
    xcb = xc.astype(BF16)
    lo, hi = xcb[:, :RG_HALF], xcb[:, RG_HALF:]
    ra = jnp.concatenate([_dot(lo, wa_ref[0]), _dot(hi, wa_ref[1])], axis=1) + ba_ref[...]
    ia = jnp.concatenate([_dot(lo, wx_ref[0]), _dot(hi, wx_ref[1])], axis=1) + bx_ref[...]
    rg = jax.nn.sigmoid(ra)
    ig = jax.nn.sigmoid(ia)
    log_a = (-RG_C * rg) * _softplus(-lam_ref[...])
    a = jnp.exp(log_a)
    bv = jnp.sqrt(1.0 - a * a) * (ig * xc)
    a3, b3 = _tile_scan(a.reshape(bb, tt, c), bv.reshape(bb, tt, c))
    if tt == SUBLANES:
        hs3 = a3 * hlast_ref[...] + b3
        hlast_ref[...] = hs3[:, tt - 1:tt, :]
        hs = hs3.reshape(r, c)
    else:
        a_s[...] = a3.reshape(r, c)
        b_s[...] = b3.reshape(r, c)

        def tile_step(j, h):
            r0 = pl.multiple_of(j * SUBLANES, SUBLANES)
            hs_t = a_s[pl.ds(r0, SUBLANES), :] * h + b_s[pl.ds(r0, SUBLANES), :]
            hs_s[pl.ds(r0, SUBLANES), :] = hs_t
            return hs_t[SUBLANES - 1:SUBLANES, :]

        hlast_ref[0] = lax.fori_loop(0, r // SUBLANES, tile_step, hlast_ref[0], unroll=4)
        hs = hs_s[...]
    z = (jax.nn.gelu(gate) * hs).astype(BF16)
    y = _dot(z, wout_ref[...]).reshape(bb, tt, d)
    xo_ref[...] = x + (1.0 + gt_ref[...]) * y


def _ffn_kernel(*refs, final):
    if final:
        (x_ref, g_ref, sh_ref, sc_ref, gt_ref, hist0_ref, wup_ref, cw_ref, cb_ref, wdn_ref, gf_ref,
         xo_ref, tail_ref) = refs
    else:
        (x_ref, g_ref, sh_ref, sc_ref, gt_ref, hist0_ref, wup_ref, cw_ref, cb_ref, wdn_ref,
         xo_ref, tail_ref) = refs
    bb, tt, d = x_ref.shape
    r = bb * tt

    @pl.when(pl.program_id(1) == 0)
    def _():
        tail_ref[...] = hist0_ref[...]

    x = x_ref[...]
    hn = _modnorm(x, g_ref[...][None], sh_ref[...], sc_ref[...]).reshape(r, d).astype(BF16)
    acc = jnp.zeros((r, d), F32)
    for n in range(D_FF // FFN_CHUNK):
        parts = []
        for lo in (n * FFN_CHUNK, D_FF + n * FFN_CHUNK):
            cols = slice(lo, lo + FFN_CHUNK)
            u3 = _dot(hn, wup_ref[:, cols]).reshape(bb, tt, FFN_CHUNK)
            parts.append(_causal_conv(u3, tail_ref[:, :, cols], cw_ref[:, cols], cb_ref[:, cols]))
            tail_ref[:, :, cols] = u3[:, tt - SUBLANES:, :]
        z = (jax.nn.gelu(parts[0]) * parts[1]).reshape(r, FFN_CHUNK).astype(BF16)
        acc = acc + _dot(z, wdn_ref[n * FFN_CHUNK:(n + 1) * FFN_CHUNK, :])
    xn = x + (1.0 + gt_ref[...]) * acc.reshape(bb, tt, d)
    if final:
        xn = _rmsnorm(xn, gf_ref[...][None])
    xo_ref[...] = xn


def _split3(x):
    hi = x.astype(BF16).astype(F32)
    r1 = x - hi
    mid = r1.astype(BF16).astype(F32)
    lo = (r1 - mid).astype(BF16).astype(F32)
    return hi, mid, lo


def _kv_kernel(*refs, prompt):
    if prompt:
        (x_ref, g_ref, sh_ref, sc_ref, wk_ref, wv_ref, wf_ref, bf_ref, cum0_ref, sel_ref,
         k_ref, v_ref, lf_ref, kaug_ref, vt_ref, cumt_ref, carry) = refs
    else:
        (x_ref, g_ref, sh_ref, sc_ref, wk_ref, wv_ref, wf_ref, bf_ref, cum0_ref,
         k_ref, v_ref, lf_ref, cum_ref, carry) = refs
    bb, tt, d = x_ref.shape
    r = bb * tt

    @pl.when(pl.program_id(1) == 0)
    def _():
        carry[...] = cum0_ref[...]

    hk = _modnorm(x_ref[...], g_ref[...][None], sh_ref[...], sc_ref[...]).reshape(r, d).astype(BF16)
    k = _dot(hk, wk_ref[...])
    v = _dot(hk, wv_ref[...])
    lf3 = _log_sigmoid(_dot(hk, wf_ref[...]) + bf_ref[...]).reshape(bb, tt, LANES)
    k_ref[...] = k.reshape(bb, tt, HW)
    v_ref[...] = v.reshape(bb, tt, HW)
    lf_ref[...] = lf3
    tpos = lax.broadcasted_iota(jnp.int32, (1, tt, LANES), 1)
    cs = lf3
    s = 1
    while s < tt:
        cs = cs + jnp.where(tpos >= s, pltpu.roll(cs, s, axis=1), 0.0)
        s *= 2
    cum = cs + carry[...]
    carry[...] = cum[:, tt - 1:tt, :]
    if not prompt:
        cum_ref[...] = cum
        return
    ck = cum[0]
    pieces = jnp.concatenate(_split3(ck), axis=1).astype(BF16)
    kx = _dot(pieces, sel_ref[...])
    lane = lax.broadcasted_iota(jnp.int32, (1, kx.shape[1]), 1) % LANES
    kx = jnp.where((lane >= 2 * N_SPLIT) & (lane < 3 * N_SPLIT), 1.0, kx).astype(BF16)
    kb = k.astype(BF16)
    for p in range(N_HEADS // 2):
        kaug_ref[0, :, 2 * p * LANES:(2 * p + 1) * LANES] = kb[:, p * LANES:(p + 1) * LANES]
        kaug_ref[0, :, (2 * p + 1) * LANES:(2 * p + 2) * LANES] = kx[:, p * LANES:(p + 1) * LANES]
    vt_ref[0] = v.T.astype(BF16)
    cumt_ref[0] = ck.T[:N_HEADS, :]


def _qproj_kernel(x_ref, g_ref, sh_ref, sc_ref, wq_ref, q_ref, *, transposed):
    bb, tt, d = x_ref.shape
    hn = _modnorm(x_ref[...], g_ref[...][None], sh_ref[...], sc_ref[...])
    q = _dot(hn.reshape(bb * tt, d).astype(BF16), wq_ref[...]) * (HEAD_DIM ** -0.5)
    if transposed:
        q_ref[0] = q.T.astype(q_ref.dtype)
    else:
        q_ref[...] = q.reshape(bb, tt, HW).astype(q_ref.dtype)


def _oproj_kernel(x_ref, o_ref, gt_ref, wo_ref, xo_ref, *, transposed):
    bb, tt, d = x_ref.shape
    if transposed:
        y = lax.dot_general(o_ref[0], wo_ref[...], (((0,), (0,)), ((), ())), preferred_element_type=F32)
    else:
        y = _dot(o_ref[...].reshape(bb * tt, HW).astype(BF16), wo_ref[...])
    xo_ref[...] = x_ref[...] + (1.0 + gt_ref[...]) * y.reshape(bb, tt, d)


def _flash_kernel(qi_ref, ki_ref, kaug_ref, qt_ref, vt_ref, cumt_ref, o_ref, qa_s, m_s, acc_s):
    group = pl.program_id(1)
    step_id = pl.program_id(2)
    qi = qi_ref[step_id]
    ki = ki_ref[step_id]
    n_hp, _, tq = qa_s.shape
    tk = kaug_ref.shape[1]
    row = lax.broadcasted_iota(jnp.int32, (LANES, 1), 0)

    @pl.when(ki == 0)
    def _():
        m_s[...] = jnp.full(m_s.shape, -jnp.inf, F32)
        acc_s[...] = jnp.zeros(acc_s.shape, F32)
        for hh in range(n_hp):
            pr, j = divmod(hh, 2)
            qt = qt_ref[0, pr * LANES:(pr + 1) * LANES, :]
            in_head = (row >= j * HEAD_DIM) & (row < (j + 1) * HEAD_DIM)
            qa_s[hh, :LANES, :] = jnp.where(in_head, qt, jnp.zeros_like(qt))
            hi, mid, lo = _split3(cumt_ref[0, pl.ds(group * n_hp + hh, 1), :])
            pick = ((row >= j * N_SPLIT) & (row < (j + 1) * N_SPLIT)).astype(F32)
            ex = jnp.where(row == 2 * N_SPLIT, hi,
                           jnp.where(row == 2 * N_SPLIT + 1, mid,
                                     jnp.where(row == 2 * N_SPLIT + 2, lo, pick)))
            qa_s[hh, LANES:, :] = ex.astype(BF16)

    def step(masked):
        for hh in range(n_hp):
            pr, j = divmod(hh, 2)
            ka = kaug_ref[0, :, 2 * pr * LANES:(2 * pr + 2) * LANES]
            s = _dot(ka, qa_s[hh])
            if masked:
                kpos = lax.broadcasted_iota(jnp.int32, (tk, tq), 0)
                qpos = lax.broadcasted_iota(jnp.int32, (tk, tq), 1)
                s = jnp.where(kpos <= qpos, s, -jnp.inf)
            m_prev = m_s[hh]
            m_new = jnp.maximum(m_prev, jnp.max(s, axis=0, keepdims=True))
            alpha = jnp.exp(m_prev - m_new)
            p = jnp.exp(s - m_new).astype(BF16)
            vt = vt_ref[0, pr * LANES:(pr + 1) * LANES, :]
            in_head = (row >= j * HEAD_DIM) & (row < (j + 1) * HEAD_DIM)
            vta = jnp.where(in_head, vt, jnp.ones_like(vt))
            acc_s[hh] = alpha * acc_s[hh] + _dot(vta, p)
            m_s[hh] = m_new

    @pl.when(ki < qi)
    def _():
        step(False)

    @pl.when(ki == qi)
    def _():
        step(True)
        for pr in range(n_hp // 2):
            a0 = acc_s[2 * pr]
            a1 = acc_s[2 * pr + 1]
            o0 = a0[:HEAD_DIM] / a0[HEAD_DIM:HEAD_DIM + 1]
            o1 = a1[HEAD_DIM:] / a1[0:1]
            o_ref[0, pr * LANES:(pr + 1) * LANES, :] = jnp.concatenate([o0, o1], axis=0).astype(o_ref.dtype)


def _paged_cum_kernel(pt_ref, *refs):
    n_pages = len(refs) - 2
    ck_ref, tot_ref = refs[n_pages:]
    cs = jnp.concatenate([refs[n][0] for n in range(n_pages)], axis=0)
    lane = lax.broadcasted_iota(jnp.int32, cs.shape, 1)
    s = 1
    while s < PAGE_SIZE:
        cs = cs + jnp.where(lane >= s, pltpu.roll(cs, s, axis=1), 0.0)
        s *= 2
    totals = jnp.broadcast_to(cs[:, PAGE_SIZE - 1:PAGE_SIZE], cs.shape)
    carry = jnp.zeros((N_HEADS, PAGE_SIZE), F32)
    for n in range(n_pages):
        rows = slice(n * N_HEADS, (n + 1) * N_HEADS)
        ck_ref[0, :, n * PAGE_SIZE:(n + 1) * PAGE_SIZE] = cs[rows] + carry
        carry = carry + totals[rows]
    tot_ref[0] = carry


def _decode_attn_kernel(pt_ref, *refs, pages_per_step):
    pps = pages_per_step
    q_ref = refs[0]
    kt_refs = refs[1:1 + pps]
    vt_refs = refs[1 + pps:1 + 2 * pps]
    ck_ref, cq_ref, kn_ref, vn_ref, ckn_ref, o_ref, qbd_s, m_s, l_s, acc_s = refs[1 + 2 * pps:]
    j = pl.program_id(1)
    rows = qbd_s.shape[0]
    nt = rows // N_HEADS
    nk = pps * PAGE_SIZE
    head_mask = (lax.broadcasted_iota(jnp.int32, (N_HEADS, HW), 1) // HEAD_DIM
                 == lax.broadcasted_iota(jnp.int32, (N_HEADS, HW), 0))

    @pl.when(j == 0)
    def _():
        m_s[...] = jnp.full(m_s.shape, -jnp.inf, F32)
        l_s[...] = jnp.zeros(l_s.shape, F32)
        acc_s[...] = jnp.zeros(acc_s.shape, F32)
        q = q_ref[0]
        qbd = jnp.where(head_mask[None], q[:, None, :], 0.0)
        qbd_s[...] = qbd.reshape(rows, HW).astype(BF16)

    def update(s, pv_fn):
        m_prev = m_s[...]
        m_new = jnp.maximum(m_prev, jnp.max(s, axis=1, keepdims=True))
        alpha = jnp.exp(m_prev - m_new)
        p = jnp.exp(s - m_new)
        l_s[...] = alpha * l_s[...] + jnp.sum(p, axis=1, keepdims=True)
        acc_s[...] = alpha * acc_s[...] + pv_fn(p.astype(BF16))
        m_s[...] = m_new

    kt = jnp.concatenate([r[0].astype(BF16) for r in kt_refs], axis=1)
    vt = jnp.concatenate([r[0].astype(BF16) for r in vt_refs], axis=1)
    s = _dot(qbd_s[...], kt)
    bias = cq_ref[0].reshape(nt, N_HEADS, 1) - ck_ref[0][None]
    s = (s.reshape(nt, N_HEADS, nk) + bias).reshape(rows, nk)
    update(s, lambda p: lax.dot_general(p, vt, (((1,), (1,)), ((), ())), preferred_element_type=F32))

    @pl.when(j == pl.num_programs(1) - 1)
    def _():
        pad = jnp.zeros((PAGE_SIZE - nt, HW), F32)
        kb = jnp.concatenate([kn_ref[0], pad], axis=0).astype(BF16)
        vb = jnp.concatenate([vn_ref[0], pad], axis=0).astype(BF16)
        tok = lax.broadcasted_iota(jnp.int32, (rows, PAGE_SIZE), 0) // N_HEADS
        key = lax.broadcasted_iota(jnp.int32, (rows, PAGE_SIZE), 1)
        sn = lax.dot_general(qbd_s[...], kb, (((1,), (1,)), ((), ())), preferred_element_type=F32)
        sn = jnp.where(key <= tok, sn + cq_ref[0] - ckn_ref[0], -jnp.inf)
        update(sn, lambda p: _dot(p, vb))
        o_full = (acc_s[...] / l_s[...]).reshape(nt, N_HEADS, HW)
        o_ref[0] = jnp.sum(jnp.where(head_mask[None], o_full, 0.0), axis=1)


def _resident(shape):
    nd = len(shape)
    return pl.BlockSpec(shape, lambda *_: (0,) * nd, pipeline_mode=pl.Buffered(1))


def _tok_spec(bb, tt, width):
    return pl.BlockSpec((bb, tt, width), lambda i, t: (i, t, 0))


def _row_spec(bb, rows, width):
    return pl.BlockSpec((bb, rows, width), lambda i, t: (i, 0, 0))


def _params(sem):
    return pltpu.CompilerParams(dimension_semantics=sem, vmem_limit_bytes=VMEM_LIMIT)


def _mod_call(c_all, w, b):
    n_layers, d, n = w.shape
    tn = 1536 if n % 1536 == 0 else 1024
    rows = c_all.shape[0]
    return pl.pallas_call(
        _mod_kernel,
        grid=(n_layers, n // tn),
        in_specs=[pl.BlockSpec((rows, d), lambda l, j: (0, 0)),
                  pl.BlockSpec((1, d, tn), lambda l, j: (l, 0, j)),
                  pl.BlockSpec((1, 1, tn), lambda l, j: (l, 0, j))],
        out_specs=pl.BlockSpec((1, rows, tn), lambda l, j: (l, 0, j)),
        out_shape=jax.ShapeDtypeStruct((n_layers, rows, n), F32),
        compiler_params=_params(("arbitrary", "arbitrary")),
        name="mod_proj",
    )(c_all, w, b.reshape(n_layers, 1, n))


def _rglru_call(x, mods, hist0, h0, w, bb, tt):
    b, t, d = x.shape
    c = D_RNN
    r = bb * tt
    sh, sc, gt = mods
    scratch_rows = r if tt != SUBLANES else SUBLANES
    return pl.pallas_call(
        _rglru_kernel,
        grid=(b // bb, t // tt),
        in_specs=[_tok_spec(bb, tt, d), _resident((1, d)),
                  _row_spec(bb, 1, d), _row_spec(bb, 1, d), _row_spec(bb, 1, d),
                  _row_spec(bb, SUBLANES, c), _row_spec(bb, 1, c),
                  _resident((d, 2 * c)), _resident(w["cw"].shape), _resident((1, c)),
                  _resident((2, RG_HALF, RG_HALF)), _resident((2, RG_HALF, RG_HALF)),
                  _resident((1, c)), _resident((1, c)), _resident((1, c)), _resident((c, d))],
        out_specs=[_tok_spec(bb, tt, d), _row_spec(bb, SUBLANES, c), _row_spec(bb, 1, c)],
        out_shape=[jax.ShapeDtypeStruct((b, t, d), F32),
                   jax.ShapeDtypeStruct((b, SUBLANES, c), F32),
                   jax.ShapeDtypeStruct((b, 1, c), F32)],
        scratch_shapes=[pltpu.VMEM((scratch_rows, c), F32)] * 3,
        compiler_params=_params(("arbitrary", "arbitrary")),
        name="rglru_mixer",
    )(x, w["g"], sh, sc, gt, hist0, h0, w["win"], w["cw"], w["cb"], w["wa"], w["wx"],
      w["ba"], w["bx"], w["lam"], w["wout"])


def _ffn_call(x, mods, hist0, w, bb, tt, final_g=None):
    b, t, d = x.shape
    sh, sc, gt = mods
    final = final_g is not None
    in_specs = [_tok_spec(bb, tt, d), _resident((1, d)),
                _row_spec(bb, 1, d), _row_spec(bb, 1, d), _row_spec(bb, 1, d),
                _row_spec(bb, SUBLANES, 2 * D_FF),
                _resident((d, 2 * D_FF)), _resident(w["cw"].shape), _resident((1, 2 * D_FF)),
                _resident((D_FF, d))]
    args = [x, w["g"], sh, sc, gt, hist0, w["wup"], w["cw"], w["cb"], w["wdn"]]
    if final:
        in_specs.append(_resident((1, d)))
        args.append(final_g)
    return pl.pallas_call(
        functools.partial(_ffn_kernel, final=final),
        grid=(b // bb, t // tt),
        in_specs=in_specs,
        out_specs=[_tok_spec(bb, tt, d), _row_spec(bb, SUBLANES, 2 * D_FF)],
        out_shape=[jax.ShapeDtypeStruct((b, t, d), F32),
                   jax.ShapeDtypeStruct((b, SUBLANES, 2 * D_FF), F32)],
        compiler_params=_params(("arbitrary", "arbitrary")),
        name="conv_ffn",
    )(*args)


def _kv_call(x, g, shift, scale, w, cum0, bb, tt, prompt):
    b, t, d = x.shape
    in_specs = [_tok_spec(bb, tt, d), _resident((1, d)), _row_spec(bb, 1, d), _row_spec(bb, 1, d),
                _resident((d, HW)), _resident((d, HW)), _resident((d, LANES)), _resident((1, LANES)),
                _row_spec(bb, 1, LANES)]
    out_specs = [_tok_spec(bb, tt, HW), _tok_spec(bb, tt, HW), _tok_spec(bb, tt, LANES)]
    out_shape = [jax.ShapeDtypeStruct((b, t, HW), F32), jax.ShapeDtypeStruct((b, t, HW), F32),
                 jax.ShapeDtypeStruct((b, t, LANES), F32)]
    args = [x, g, shift, scale, w["wk"], w["wv"], w["wf"], w["bf"], cum0]
    if prompt:
        in_specs.append(_resident(w["sel"].shape))
        args.append(w["sel"])
        out_specs += [_tok_spec(1, tt, 2 * HW), pl.BlockSpec((1, HW, tt), lambda i, j: (i, 0, j)),
                      pl.BlockSpec((1, N_HEADS, tt), lambda i, j: (i, 0, j))]
        out_shape += [jax.ShapeDtypeStruct((b, t, 2 * HW), BF16), jax.ShapeDtypeStruct((b, HW, t), BF16),
                      jax.ShapeDtypeStruct((b, N_HEADS, t), F32)]
    else:
        out_specs.append(_tok_spec(bb, tt, LANES))
        out_shape.append(jax.ShapeDtypeStruct((b, t, LANES), F32))
    return pl.pallas_call(
        functools.partial(_kv_kernel, prompt=prompt),
        grid=(b // bb, t // tt),
        in_specs=in_specs, out_specs=out_specs, out_shape=out_shape,
        scratch_shapes=[pltpu.VMEM((bb, 1, LANES), F32)],
        compiler_params=_params(("arbitrary", "arbitrary")),
        name="kv_proj",
    )(*args)


def _qproj_call(x, g, mods, wq, bb, tt, transposed):
    b, t, d = x.shape
    sh, sc, _ = mods
    if transposed:
        out_spec = pl.BlockSpec((1, HW, tt), lambda i, j: (i, 0, j))
        out_shape = jax.ShapeDtypeStruct((b, HW, t), BF16)
    else:
        out_spec = _tok_spec(bb, tt, HW)
        out_shape = jax.ShapeDtypeStruct((b, t, HW), F32)
    return pl.pallas_call(
        functools.partial(_qproj_kernel, transposed=transposed),
        grid=(b // bb, t // tt),
        in_specs=[_tok_spec(bb, tt, d), _resident((1, d)), _row_spec(bb, 1, d), _row_spec(bb, 1, d),
                  _resident((d, HW))],
        out_specs=out_spec,
        out_shape=out_shape,
        compiler_params=_params(("arbitrary", "arbitrary")),
        name="q_proj",
    )(x, g, sh, sc, wq)


def _oproj_call(x, o, gt, wo, bb, tt, transposed):
    b, t, d = x.shape
    o_spec = pl.BlockSpec((1, HW, tt), lambda i, j: (i, 0, j)) if transposed else _tok_spec(bb, tt, HW)
    return pl.pallas_call(
        functools.partial(_oproj_kernel, transposed=transposed),
        grid=(b // bb, t // tt),
        in_specs=[_tok_spec(bb, tt, d), o_spec, _row_spec(bb, 1, d), _resident((HW, d))],
        out_specs=_tok_spec(bb, tt, d),
        out_shape=jax.ShapeDtypeStruct((b, t, d), F32),
        compiler_params=_params(("arbitrary", "arbitrary")),
        name="o_proj",
    )(x, o, gt, wo)


def _flash_call(kaug, qt, vt, cumt):
    b, hw, t = qt.shape
    tq = FLASH_BLOCK
    nq = t // tq
    hp = FLASH_HEADS
    pairs = [(qi, ki) for qi in range(nq) for ki in range(qi + 1)]
    qi_tab = jnp.asarray([p[0] for p in pairs], jnp.int32)
    ki_tab = jnp.asarray([p[1] for p in pairs], jnp.int32)
    return pl.pallas_call(
        _flash_kernel,
        grid_spec=pltpu.PrefetchScalarGridSpec(
            num_scalar_prefetch=2,
            grid=(b, N_HEADS // hp, len(pairs)),
            in_specs=[pl.BlockSpec((1, tq, hp * LANES), lambda i, g, s, qi, ki: (i, ki[s], g)),
                      pl.BlockSpec((1, hp * HEAD_DIM, tq), lambda i, g, s, qi, ki: (i, g, qi[s])),
                      pl.BlockSpec((1, hp * HEAD_DIM, tq), lambda i, g, s, qi, ki: (i, g, ki[s])),
                      pl.BlockSpec((1, N_HEADS, tq), lambda i, g, s, qi, ki: (i, 0, qi[s]))],
            out_specs=pl.BlockSpec((1, hp * HEAD_DIM, tq), lambda i, g, s, qi, ki: (i, g, qi[s])),
            scratch_shapes=[pltpu.VMEM((hp, 2 * LANES, tq), BF16), pltpu.VMEM((hp, 1, tq), F32),
                            pltpu.VMEM((hp, LANES, tq), F32)]),
        out_shape=jax.ShapeDtypeStruct((b, hw, t), BF16),
        compiler_params=_params(("arbitrary", "arbitrary", "arbitrary")),
        name="fox_prefill",
    )(qi_tab, ki_tab, kaug, qt, vt, cumt)


def _page_spec(rows, n, per_step):
    return pl.BlockSpec((1, rows, PAGE_SIZE), lambda i, j, pt: (pt[i, j * per_step + n], 0, 0))


def _paged_cum_call(page_table, lft):
    b, n_pages = page_table.shape
    return pl.pallas_call(
        _paged_cum_kernel,
        grid_spec=pltpu.PrefetchScalarGridSpec(
            num_scalar_prefetch=1,
            grid=(b, 1),
            in_specs=[_page_spec(N_HEADS, n, n_pages) for n in range(n_pages)],
            out_specs=[pl.BlockSpec((1, N_HEADS, n_pages * PAGE_SIZE), lambda i, j, pt: (i, 0, 0)),
                       pl.BlockSpec((1, N_HEADS, PAGE_SIZE), lambda i, j, pt: (i, 0, 0))]),
        out_shape=[jax.ShapeDtypeStruct((b, N_HEADS, n_pages * PAGE_SIZE), F32),
                   jax.ShapeDtypeStruct((b, N_HEADS, PAGE_SIZE), F32)],
        compiler_params=_params(("arbitrary", "arbitrary")),
        name="paged_logf_cumsum",
    )(page_table, *([lft] * n_pages))


def _decode_attn_call(page_table, q, ktp, vtp, ckt, cq_col, k_new, v_new, ck_new):
    b, nt, _ = q.shape
    n_pages = page_table.shape[1]
    pps = DECODE_PAGES
    rows = nt * N_HEADS
    fixed = lambda shape: pl.BlockSpec(shape, lambda i, j, pt: (i, 0, 0))
    return pl.pallas_call(
        functools.partial(_decode_attn_kernel, pages_per_step=pps),
        grid_spec=pltpu.PrefetchScalarGridSpec(
            num_scalar_prefetch=1,
            grid=(b, n_pages // pps),
            in_specs=([fixed((1, nt, HW))]
                      + [_page_spec(HW, n, pps) for n in range(pps)] * 2
                      + [pl.BlockSpec((1, N_HEADS, pps * PAGE_SIZE), lambda i, j, pt: (i, 0, j)),
                         fixed((1, rows, 1)), fixed((1, nt, HW)), fixed((1, nt, HW)),
                         fixed((1, rows, PAGE_SIZE))]),
            out_specs=fixed((1, nt, HW)),
            scratch_shapes=[pltpu.VMEM((rows, HW), BF16), pltpu.VMEM((rows, 1), F32),
                            pltpu.VMEM((rows, 1), F32), pltpu.VMEM((rows, HW), F32)]),
        out_shape=jax.ShapeDtypeStruct((b, nt, HW), F32),
        compiler_params=_params(("arbitrary", "arbitrary")),
        name="fox_decode",
    )(page_table, q, *([ktp] * pps), *([vtp] * pps), ckt, cq_col, k_new, v_new, ck_new)


def _block_diag_halves(w):
    per = RG_BLOCKS // 2
    eye = jnp.eye(per, dtype=w.dtype)
    w2 = w.reshape(2, per, RG_BLOCK_W, RG_BLOCK_W)
    dense = w2[:, :, :, None, :] * eye[None, :, None, :, None]
    return dense.reshape(2, RG_HALF, RG_HALF)


def _prep_weights(p):
    row = lambda a: a.reshape(1, -1)
    layers = []
    for l in range(DEPTH):
        w = {"mix_g": row(p["norm_mix_g"][l]),
             "ffn": {"g": row(p["norm_ffn_g"][l]), "wup": p["ffn_w_up"][l].astype(BF16),
                     "cw": p["ffn_conv_w"][l], "cb": row(p["ffn_conv_b"][l]),
                     "wdn": p["ffn_w_down"][l].astype(BF16)}}
        if l < N_A_LAYERS:
            w["rg"] = {"g": w["mix_g"], "win": p["rg_w_in"][l].astype(BF16), "cw": p["rg_conv_w"][l],
                       "cb": row(p["rg_conv_b"][l]),
                       "wa": _block_diag_halves(p["rg_wa"][l]).astype(BF16),
                       "wx": _block_diag_halves(p["rg_wx"][l]).astype(BF16),
                       "ba": row(p["rg_ba"][l]), "bx": row(p["rg_bx"][l]), "lam": row(p["rg_lambda"][l]),
                       "wout": p["rg_w_out"][l].astype(BF16)}
        else:
            j = l - N_A_LAYERS
            w["wq"] = p["fa_wq"][j].astype(BF16)
            w["wo"] = p["fa_wo"][j].astype(BF16)
        layers.append(w)
    kvw = p["kv_w"]
    kv = {"g": row(p["kv_norm_g"]), "wk": kvw[:, :HW].astype(BF16), "wv": kvw[:, HW:2 * HW].astype(BF16),
          "wf": jnp.pad(kvw[:, 2 * HW:], ((0, 0), (0, LANES - N_HEADS))).astype(BF16),
          "bf": jnp.pad(p["kv_b_f"], (0, LANES - N_HEADS)).reshape(1, LANES),
          "sel": _cum_selector()}
    return layers, kv


def _cum_selector():
    piece = jnp.arange(N_SPLIT * LANES) // LANES
    head = jnp.arange(N_SPLIT * LANES) % LANES
    target = (head // 2) * LANES + N_SPLIT * (head % 2) + piece
    hit = (jnp.arange(N_HEADS // 2 * LANES)[None, :] == target[:, None]) & (head[:, None] < N_HEADS)
    return jnp.where(hit, -1.0, 0.0).astype(BF16)


def _trunk(x, mod, kv_mod, rg_conv_buf, rg_h0, ffn_buf, layers, kvw, final_g, bb, tt, past):
    bsz, t, d = x.shape
    prompt = past is None
    lead = SUBLANES - rg_conv_buf.shape[2]
    rg_hist = jnp.pad(rg_conv_buf, ((0, 0), (0, 0), (lead, 0), (0, 0)))
    lead_f = SUBLANES - ffn_buf.shape[2]
    ffn_hist = jnp.pad(ffn_buf, ((0, 0), (0, 0), (lead_f, 0), (0, 0)))
    rg_conv_new, rg_h_new, ffn_new = [], [], []
    for l in range(DEPTH):
        w = layers[l]
        m6 = [mod[l][:, i * d:(i + 1) * d].reshape(bsz, 1, d) for i in range(6)]
        mix_mods, ffn_mods = m6[:3], m6[3:]
        if l < N_A_LAYERS:
            x, tail, hl = _rglru_call(x, mix_mods, rg_hist[l], rg_h0[l].reshape(bsz, 1, D_RNN),
                                      w["rg"], bb, tt)
            rg_conv_new.append(tail[:, lead:, :])
            rg_h_new.append(hl.reshape(bsz, D_RNN))
        else:
            q = _qproj_call(x, w["mix_g"], mix_mods, w["wq"], bb, tt, prompt)
            if prompt:
                o = _flash_call(kaug, q, vt, cumt)
            else:
                o = _decode_attn_call(past["page_table"], q, past["ktp"], past["vtp"], past["ckt"],
                                      cq_col, k_new, v_new, ck_new_rows)
            x = _oproj_call(x, o, mix_mods[2], w["wo"], bb, tt, prompt)
        fin = final_g if l == DEPTH - 1 else None
        x, ftail = _ffn_call(x, ffn_mods, ffn_hist[l], w["ffn"], bb, tt, fin)
        ffn_new.append(ftail[:, lead_f:, :])
        if l == N_A_LAYERS - 1:
            kv_shift = kv_mod[:, :d].reshape(bsz, 1, d)
            kv_scale = kv_mod[:, d:].reshape(bsz, 1, d)
            if prompt:
                cum0 = jnp.zeros((bsz, 1, LANES), F32)
                k_new, v_new, lf, kaug, vt, cumt = _kv_call(
                    x, kvw["g"], kv_shift, kv_scale, kvw, cum0, bb, tt, True)
            else:
                cum0 = jnp.pad(past["total"], ((0, 0), (0, LANES - N_HEADS))).reshape(bsz, 1, LANES)
                k_new, v_new, lf, cum = _kv_call(x, kvw["g"], kv_shift, kv_scale, kvw, cum0, bb, tt, False)
                cum_new = cum[:, :, :N_HEADS]
                cq_col = cum_new.reshape(bsz, t * N_HEADS, 1)
                ck_t = jnp.transpose(cum_new, (0, 2, 1))
                ck_t = jnp.pad(ck_t, ((0, 0), (0, 0), (0, PAGE_SIZE - t)))
                ck_new_rows = jnp.tile(ck_t, (1, t, 1))
    k4 = k_new.reshape(bsz, t, N_HEADS, HEAD_DIM)
    v4 = v_new.reshape(bsz, t, N_HEADS, HEAD_DIM)
    return (x, k4, v4, lf[:, :, :N_HEADS], jnp.stack(rg_conv_new), jnp.stack(rg_h_new), jnp.stack(ffn_new))


def kernel(x_prompt, x_sample, c_prompt, c_sample, cache_k, cache_v, cache_logf, page_table,
           state_rglru_conv, state_rglru_h, state_ffn_conv, mod_w, mod_b, norm_mix_g, norm_ffn_g,
           rg_w_in, rg_conv_w, rg_conv_b, rg_wa, rg_ba, rg_wx, rg_bx, rg_lambda, rg_w_out,
           kv_norm_g, kv_mod_w, kv_mod_b, kv_w, kv_b_f, fa_wq, fa_wo,
           ffn_w_up, ffn_conv_w, ffn_conv_b, ffn_w_down, final_norm_g):
    p = dict(norm_mix_g=norm_mix_g, norm_ffn_g=norm_ffn_g, rg_w_in=rg_w_in, rg_conv_w=rg_conv_w,
             rg_conv_b=rg_conv_b, rg_wa=rg_wa, rg_ba=rg_ba, rg_wx=rg_wx, rg_bx=rg_bx,
             rg_lambda=rg_lambda, rg_w_out=rg_w_out, kv_norm_g=kv_norm_g, kv_w=kv_w, kv_b_f=kv_b_f,
             fa_wq=fa_wq, fa_wo=fa_wo, ffn_w_up=ffn_w_up, ffn_conv_w=ffn_conv_w,
             ffn_conv_b=ffn_conv_b, ffn_w_down=ffn_w_down)
    layers, kvw = _prep_weights(p)
    final_g = final_norm_g.reshape(1, D_MODEL)

    nb_p, nb_s = x_prompt.shape[0], x_sample.shape[0]
    n_all = nb_p + nb_s
    rows = -(-n_all // SUBLANES) * SUBLANES
    c_all = jnp.pad(jnp.concatenate([c_prompt, c_sample], axis=0), ((0, rows - n_all), (0, 0)))
    mod_all = _mod_call(c_all, mod_w, mod_b)
    kv_mod_all = _mod_call(c_all, kv_mod_w[None], kv_mod_b[None])[0]

    zero_rc = jnp.zeros((N_A_LAYERS, nb_p) + state_rglru_conv.shape[2:], F32)
    zero_rh = jnp.zeros((N_A_LAYERS, nb_p, D_RNN), F32)
    zero_fc = jnp.zeros((DEPTH, nb_p) + state_ffn_conv.shape[2:], F32)
    out_p = _trunk(x_prompt, mod_all[:, :nb_p], kv_mod_all[:nb_p], zero_rc, zero_rh, zero_fc,
                   layers, kvw, final_g, 1, 256, None)

    n_phys = cache_k.shape[0]
    ktp = jnp.transpose(cache_k, (0, 2, 3, 1)).reshape(n_phys, HW, PAGE_SIZE)
    vtp = jnp.transpose(cache_v, (0, 2, 3, 1)).reshape(n_phys, HW, PAGE_SIZE)
    lft = jnp.transpose(cache_logf, (0, 2, 1))
    ckt, tot = _paged_cum_call(page_table, lft)
    past = {"page_table": page_table, "ktp": ktp, "vtp": vtp, "ckt": ckt, "total": tot[:, :, 0]}
    out_s = _trunk(x_sample, mod_all[:, nb_p:n_all], kv_mod_all[nb_p:n_all], state_rglru_conv,
                   state_rglru_h, state_ffn_conv, layers, kvw, final_g, 32, SUBLANES, past)
    return (out_p[0], out_s[0]) + out_p[1:] + out_s[1:]
```

```python
import functools

import jax
import jax.numpy as jnp
from jax import lax
from jax.experimental import pallas as pl
from jax.experimental.pallas import tpu as pltpu

D_MODEL = 1024
DEPTH = 4
N_A_LAYERS = 2
D_RNN = 1280
RG_BLOCKS = 16
RG_BLOCK_W = D_RNN // RG_BLOCKS
RG_HALF = D_RNN // 2
RG_C = 8.0
N_HEADS = 16
HEAD_DIM = 64
HW = N_HEADS * HEAD_DIM
D_FF = 3072
EPS = 1e-6
PAGE_SIZE = 128
LANES = 128
SUBLANES = 8
FFN_CHUNK = 512
N_SPLIT = 3
LOG2E = 1.4426950408889634
ONES_ROWS = 16
FLASH_BLOCK = 512
FLASH_HEADS = 8
DECODE_PAGES = 8
VMEM_LIMIT = 56 * 1024 * 1024

F32 = jnp.float32
BF16 = jnp.bfloat16


def _rmsnorm(x, g):
    ms = jnp.mean(x * x, axis=-1, keepdims=True)
    return x * lax.rsqrt(ms + EPS) * g


def _modnorm(x, g, shift, scale):
    return _rmsnorm(x, g) * (1.0 + scale) + shift


def _softplus(z):
    return jnp.maximum(z, 0.0) + jnp.log1p(jnp.exp(-jnp.abs(z)))


def _log_sigmoid(z):
    return -_softplus(-z)


def _dot(a, b):
    return jnp.dot(a, b, preferred_element_type=F32)


def _causal_conv(x3, prev8, w, b):
    width = w.shape[0]
    bb, tt, c = x3.shape
    assert tt == SUBLANES or bb == 1
    tiles = x3.reshape(bb * tt // SUBLANES, SUBLANES, c)
    t8 = lax.broadcasted_iota(jnp.int32, (1, SUBLANES, c), 1)
    y = None
    for k in range(width):
        s = width - 1 - k
        if s == 0:
            xs = tiles
        else:
            rot = pltpu.roll(tiles, s, axis=1)
            rot_prev = pltpu.roll(prev8, s, axis=1)
            if tt != SUBLANES:
                rot_prev = jnp.concatenate([rot_prev, rot[:-1]], axis=0)
            xs = jnp.where(t8 < s, rot_prev, rot)
        term = xs * w[k:k + 1][None]
        y = term if y is None else y + term
    return (y + b[None]).reshape(bb, tt, c)


def _tile_scan(a3, b3):
    shape = a3.shape
    c = shape[-1]
    a3 = a3.reshape(-1, SUBLANES, c)
    b3 = b3.reshape(-1, SUBLANES, c)
    t8 = lax.broadcasted_iota(jnp.int32, (1, SUBLANES, c), 1)
    for s in (1, 2, 4):
        keep = t8 >= s
        a_sh = jnp.where(keep, pltpu.roll(a3, s, axis=1), 1.0)
        b_sh = jnp.where(keep, pltpu.roll(b3, s, axis=1), 0.0)
        b3 = a3 * b_sh + b3
        a3 = a3 * a_sh
    return a3.reshape(shape), b3.reshape(shape)


def _mod_kernel(c_ref, w_ref, b_ref, o_ref):
    c = c_ref[...]
    act = c * jax.nn.sigmoid(c)
    o_ref[0] = _dot(act.astype(BF16), w_ref[0].astype(BF16)) + b_ref[0]


def _rglru_kernel(x_ref, g_ref, sh_ref, sc_ref, gt_ref, hist0_ref, h0_ref, win_ref, cw_ref, cb_ref,
                  wa_ref, wx_ref, ba_ref, bx_ref, lam_ref, wout_ref,
                  xo_ref, tail_ref, hlast_ref):
    bb, tt, d = x_ref.shape
    c = D_RNN
    r = bb * tt
    hw = RG_HALF

    @pl.when(pl.program_id(1) == 0)
    def _():
        tail_ref[...] = hist0_ref[...]
        hlast_ref[...] = h0_ref[...]

    x = x_ref[...]
    hnb = _modnorm(x, g_ref[...][None], sh_ref[...], sc_ref[...]).reshape(r, d).astype(BF16)

    def project(h):
        return (_dot(hnb, win_ref[:, h * hw:(h + 1) * hw]),
                _dot(hnb, win_ref[:, c + h * hw:c + (h + 1) * hw]))

    def mix(h, gate, xb):
        cols = slice(h * hw, (h + 1) * hw)
        xb3 = xb.reshape(bb, tt, hw)
        xc3 = _causal_conv(xb3, tail_ref[:, :, cols], cw_ref[:, cols], cb_ref[:, cols])
        tail_ref[:, :, cols] = xb3[:, tt - SUBLANES:, :]
        xc = xc3.reshape(r, hw)
        xcb = xc.astype(BF16)
        rg = jax.nn.sigmoid(_dot(xcb, wa_ref[h]) + ba_ref[:, cols])
        ig = jax.nn.sigmoid(_dot(xcb, wx_ref[h]) + bx_ref[:, cols])
        log_a = (-RG_C * rg) * _softplus(-lam_ref[:, cols])
        a = jnp.exp(log_a)
        y = 1.0 - a * a
        root = jnp.where(y > 0.0, y * lax.rsqrt(y), 0.0)
        bv = root * (ig * xc)
        a3, b3 = _tile_scan(a.reshape(bb, tt, hw), bv.reshape(bb, tt, hw))
        if tt == SUBLANES:
            hs3 = a3 * hlast_ref[:, :, cols] + b3
            hlast_ref[:, :, cols] = hs3[:, tt - 1:tt, :]
            hs = hs3.reshape(r, hw)
        else:
            hrow = hlast_ref[0, :, cols]
            tiles = []
            for j in range(r // SUBLANES):
                rows = slice(j * SUBLANES, (j + 1) * SUBLANES)
                hs_t = a3[0, rows] * hrow + b3[0, rows]
                tiles.append(hs_t)
                hrow = hs_t[SUBLANES - 1:SUBLANES, :]
            hlast_ref[0, :, cols] = hrow
            hs = jnp.concatenate(tiles, axis=0)
        z = (jax.nn.gelu(gate) * hs).astype(BF16)
        return _dot(z, wout_ref[cols, :])

    halves = [project(0), project(1)]
    y = mix(0, *halves[0]) + mix(1, *halves[1])
    xo_ref[...] = x + (1.0 + gt_ref[...]) * y.reshape(bb, tt, d)


def _ffn_kernel(*refs, final):
    if final:
        (x_ref, g_ref, sh_ref, sc_ref, gt_ref, hist0_ref, wup_ref, cw_ref, cb_ref, wdn_ref, gf_ref,
         xo_ref, tail_ref) = refs
    else:
        (x_ref, g_ref, sh_ref, sc_ref, gt_ref, hist0_ref, wup_ref, cw_ref, cb_ref, wdn_ref,
         xo_ref, tail_ref) = refs
    bb, tt, d = x_ref.shape
    r = bb * tt

    @pl.when(pl.program_id(1) == 0)
    def _():
        tail_ref[...] = hist0_ref[...]

    x = x_ref[...]
    hn = _modnorm(x, g_ref[...][None], sh_ref[...], sc_ref[...]).reshape(r, d).astype(BF16)
    n_chunks = D_FF // FFN_CHUNK

    def up(n):
        return [_dot(hn, wup_ref[:, lo:lo + FFN_CHUNK]).reshape(bb, tt, FFN_CHUNK)
                for lo in (n * FFN_CHUNK, D_FF + n * FFN_CHUNK)]

    acc = jnp.zeros((r, d), F32)
    u_next = up(0)
    for n in range(n_chunks):
        u_cur = u_next
        if n + 1 < n_chunks:
            u_next = up(n + 1)
        parts = []
        for u3, lo in zip(u_cur, (n * FFN_CHUNK, D_FF + n * FFN_CHUNK)):
            cols = slice(lo, lo + FFN_CHUNK)
            parts.append(_causal_conv(u3, tail_ref[:, :, cols], cw_ref[:, cols], cb_ref[:, cols]))
            tail_ref[:, :, cols] = u3[:, tt - SUBLANES:, :]
        z = (jax.nn.gelu(parts[0]) * parts[1]).reshape(r, FFN_CHUNK).astype(BF16)
        acc = acc + _dot(z, wdn_ref[n * FFN_CHUNK:(n + 1) * FFN_CHUNK, :])
    xn = x + (1.0 + gt_ref[...]) * acc.reshape(bb, tt, d)
    if final:
        xn = _rmsnorm(xn, gf_ref[...][None])
    xo_ref[...] = xn


def _split3(x):
    hi = x.astype(BF16).astype(F32)
    r1 = x - hi
    mid = r1.astype(BF16).astype(F32)
    lo = (r1 - mid).astype(BF16).astype(F32)
    return hi, mid, lo


def _kv_kernel(*refs, prompt):
    if prompt:
        (x_ref, g_ref, sh_ref, sc_ref, wk_ref, wv_ref, wf_ref, bf_ref, cum0_ref, sel_ref,
         k_ref, v_ref, lf_ref, kaug_ref, vt_ref, cumt_ref, carry) = refs
    else:
        (x_ref, g_ref, sh_ref, sc_ref, wk_ref, wv_ref, wf_ref, bf_ref, cum0_ref,
         k_ref, v_ref, lf_ref, cum_ref, carry) = refs
    bb, tt, d = x_ref.shape
    r = bb * tt

    @pl.when(pl.program_id(1) == 0)
    def _():
        carry[...] = cum0_ref[...]

    hk = _modnorm(x_ref[...], g_ref[...][None], sh_ref[...], sc_ref[...]).reshape(r, d).astype(BF16)
    k = _dot(hk, wk_ref[...])
    v = _dot(hk, wv_ref[...])
    lf3 = _log_sigmoid(_dot(hk, wf_ref[...]) + bf_ref[...]).reshape(bb, tt, LANES)
    k_ref[...] = k.reshape(bb, tt, HW)
    v_ref[...] = v.reshape(bb, tt, HW)
    lf_ref[...] = lf3
    tpos = lax.broadcasted_iota(jnp.int32, (1, tt, LANES), 1)
    cs = lf3
    s = 1
    while s < tt:
        cs = cs + jnp.where(tpos >= s, pltpu.roll(cs, s, axis=1), 0.0)
        s *= 2
    cum = cs + carry[...]
    carry[...] = cum[:, tt - 1:tt, :]
    if not prompt:
        cum_ref[...] = cum
        return
    ck = cum[0] * LOG2E
    pieces = jnp.concatenate(_split3(ck), axis=1).astype(BF16)
    kx = _dot(pieces, sel_ref[...])
    lane = lax.broadcasted_iota(jnp.int32, (1, kx.shape[1]), 1) % LANES
    kx = jnp.where((lane >= 2 * N_SPLIT) & (lane < 3 * N_SPLIT), 1.0, kx).astype(BF16)
    kb = k.astype(BF16)
    for p in range(N_HEADS // 2):
        kaug_ref[0, :, 2 * p * LANES:(2 * p + 1) * LANES] = kb[:, p * LANES:(p + 1) * LANES]
        kaug_ref[0, :, (2 * p + 1) * LANES:(2 * p + 2) * LANES] = kx[:, p * LANES:(p + 1) * LANES]
    vt_ref[0] = v.T.astype(BF16)
    cumt_ref[0] = ck.T[:N_HEADS, :]


def _qproj_kernel(x_ref, g_ref, sh_ref, sc_ref, wq_ref, q_ref, *, transposed):
    bb, tt, d = x_ref.shape
    hn = _modnorm(x_ref[...], g_ref[...][None], sh_ref[...], sc_ref[...])
    q = _dot(hn.reshape(bb * tt, d).astype(BF16), wq_ref[...]) * (HEAD_DIM ** -0.5)
    if transposed:
        q_ref[0] = (q * LOG2E).T.astype(q_ref.dtype)
    else:
        q_ref[...] = q.reshape(bb, tt, HW).astype(q_ref.dtype)


def _oproj_kernel(x_ref, o_ref, gt_ref, wo_ref, xo_ref, *, transposed):
    bb, tt, d = x_ref.shape
    if transposed:
        y = lax.dot_general(o_ref[0], wo_ref[...], (((0,), (0,)), ((), ())), preferred_element_type=F32)
    else:
        y = _dot(o_ref[...].reshape(bb * tt, HW).astype(BF16), wo_ref[...])
    xo_ref[...] = x_ref[...] + (1.0 + gt_ref[...]) * y.reshape(bb, tt, d)


def _flash_kernel(qi_ref, ki_ref, kaug_ref, qt_ref, vt_ref, cumt_ref, o_ref, qa_s, m_s, acc_s):
    group = pl.program_id(1)
    step_id = pl.program_id(2)
    qi = qi_ref[step_id]
    ki = ki_ref[step_id]
    n_hp, _, tq = qa_s.shape
    tk = kaug_ref.shape[1]
    row = lax.broadcasted_iota(jnp.int32, (LANES, 1), 0)

    @pl.when(ki == 0)
    def _():
        m_s[...] = jnp.full(m_s.shape, -jnp.inf, F32)
        acc_s[...] = jnp.zeros(acc_s.shape, F32)
        for hh in range(n_hp):
            pr, j = divmod(hh, 2)
            qt = qt_ref[0, pr * LANES:(pr + 1) * LANES, :]
            in_head = (row >= j * HEAD_DIM) & (row < (j + 1) * HEAD_DIM)
            qa_s[hh, :LANES, :] = jnp.where(in_head, qt, jnp.zeros_like(qt))
            hi, mid, lo = _split3(cumt_ref[0, pl.ds(group * n_hp + hh, 1), :])
            pick = ((row >= j * N_SPLIT) & (row < (j + 1) * N_SPLIT)).astype(F32)
            ex = jnp.where(row == 2 * N_SPLIT, hi,
                           jnp.where(row == 2 * N_SPLIT + 1, mid,
                                     jnp.where(row == 2 * N_SPLIT + 2, lo, pick)))
            qa_s[hh, LANES:, :] = ex.astype(BF16)

    def step(masked):
        scores = []
        for hh in range(n_hp):
            pr, j = divmod(hh, 2)
            ka = kaug_ref[0, :, 2 * pr * LANES:(2 * pr + 2) * LANES]
            s = _dot(ka, qa_s[hh])
            if masked:
                kpos = lax.broadcasted_iota(jnp.int32, (tk, tq), 0)
                qpos = lax.broadcasted_iota(jnp.int32, (tk, tq), 1)
                s = jnp.where(kpos <= qpos, s, -jnp.inf)
            scores.append(s)
        probs = []
        for hh in range(n_hp):
            s = scores[hh]
            m_prev = m_s[hh]
            m_new = jnp.maximum(m_prev, jnp.max(s, axis=0, keepdims=True))
            probs.append((jnp.exp2(m_prev - m_new), jnp.exp2(s - m_new).astype(BF16)))
            m_s[hh] = m_new
        ones = jnp.ones((acc_s.shape[1] - HEAD_DIM, tk), BF16)
        for hh in range(n_hp):
            alpha, p = probs[hh]
            vta = jnp.concatenate([vt_ref[0, hh * HEAD_DIM:(hh + 1) * HEAD_DIM, :], ones], axis=0)
            acc_s[hh] = alpha * acc_s[hh] + _dot(vta, p)

    @pl.when(ki < qi)
    def _():
        step(False)

    @pl.when(ki == qi)
    def _():
        step(True)
        for hh in range(n_hp):
            acc = acc_s[hh]
            o = acc[:HEAD_DIM] / acc[HEAD_DIM:HEAD_DIM + 1]
            o_ref[0, hh * HEAD_DIM:(hh + 1) * HEAD_DIM, :] = o.astype(o_ref.dtype)


def _paged_cum_kernel(pt_ref, *refs):
    n_pages = len(refs) - 2
    ck_ref, tot_ref = refs[n_pages:]
    cs = jnp.concatenate([refs[n][0] for n in range(n_pages)], axis=0)
    lane = lax.broadcasted_iota(jnp.int32, cs.shape, 1)
    s = 1
    while s < PAGE_SIZE:
        cs = cs + jnp.where(lane >= s, pltpu.roll(cs, s, axis=1), 0.0)
        s *= 2
    totals = jnp.broadcast_to(cs[:, PAGE_SIZE - 1:PAGE_SIZE], cs.shape)
    carry = jnp.zeros((N_HEADS, PAGE_SIZE), F32)
    for n in range(n_pages):
        rows = slice(n * N_HEADS, (n + 1) * N_HEADS)
        ck_ref[0, :, n * PAGE_SIZE:(n + 1) * PAGE_SIZE] = cs[rows] + carry
        carry = carry + totals[rows]
    tot_ref[0] = carry


def _decode_attn_kernel(pt_ref, *refs, pages_per_step):
    pps = pages_per_step
    q_ref = refs[0]
    kt_refs = refs[1:1 + pps]
    vt_refs = refs[1 + pps:1 + 2 * pps]
    ck_ref, cq_ref, kn_ref, vn_ref, ckn_ref, o_ref, qbd_s, m_s, l_s, acc_s = refs[1 + 2 * pps:]
    j = pl.program_id(1)
    rows = qbd_s.shape[0]
    nt = rows // N_HEADS
    nk = pps * PAGE_SIZE
    head_mask = (lax.broadcasted_iota(jnp.int32, (N_HEADS, HW), 1) // HEAD_DIM
                 == lax.broadcasted_iota(jnp.int32, (N_HEADS, HW), 0))

    @pl.when(j == 0)
    def _():
        m_s[...] = jnp.full(m_s.shape, -jnp.inf, F32)
        l_s[...] = jnp.zeros(l_s.shape, F32)
        acc_s[...] = jnp.zeros(acc_s.shape, F32)
        q = q_ref[0]
        qbd = jnp.where(head_mask[None], q[:, None, :], 0.0)
        qbd_s[...] = qbd.reshape(rows, HW).astype(BF16)

    def update(s, pv_fn):
        m_prev = m_s[...]
        m_new = jnp.maximum(m_prev, jnp.max(s, axis=1, keepdims=True))
        alpha = jnp.exp(m_prev - m_new)
        p = jnp.exp(s - m_new)
        l_s[...] = alpha * l_s[...] + jnp.sum(p, axis=1, keepdims=True)
        acc_s[...] = alpha * acc_s[...] + pv_fn(p.astype(BF16))
        m_s[...] = m_new

    kt = jnp.concatenate([r[0].astype(BF16) for r in kt_refs], axis=1)
    vt = jnp.concatenate([r[0].astype(BF16) for r in vt_refs], axis=1)
    s = _dot(qbd_s[...], kt)
    bias = cq_ref[0].reshape(nt, N_HEADS, 1) - ck_ref[0][None]
    s = (s.reshape(nt, N_HEADS, nk) + bias).reshape(rows, nk)
    update(s, lambda p: lax.dot_general(p, vt, (((1,), (1,)), ((), ())), preferred_element_type=F32))

    @pl.when(j == pl.num_programs(1) - 1)
    def _():
        pad = jnp.zeros((PAGE_SIZE - nt, HW), F32)
        kb = jnp.concatenate([kn_ref[0], pad], axis=0).astype(BF16)
        vb = jnp.concatenate([vn_ref[0], pad], axis=0).astype(BF16)
        tok = lax.broadcasted_iota(jnp.int32, (rows, PAGE_SIZE), 0) // N_HEADS
        key = lax.broadcasted_iota(jnp.int32, (rows, PAGE_SIZE), 1)
        sn = lax.dot_general(qbd_s[...], kb, (((1,), (1,)), ((), ())), preferred_element_type=F32)
        sn = jnp.where(key <= tok, sn + cq_ref[0] - ckn_ref[0], -jnp.inf)
        update(sn, lambda p: _dot(p, vb))
        o_full = (acc_s[...] / l_s[...]).reshape(nt, N_HEADS, HW)
        o_ref[0] = jnp.sum(jnp.where(head_mask[None], o_full, 0.0), axis=1)


def _resident(shape):
    nd = len(shape)
    return pl.BlockSpec(shape, lambda *_: (0,) * nd, pipeline_mode=pl.Buffered(1))


def _tok_spec(bb, tt, width):
    return pl.BlockSpec((bb, tt, width), lambda i, t: (i, t, 0))


def _row_spec(bb, rows, width):
    return pl.BlockSpec((bb, rows, width), lambda i, t: (i, 0, 0))


def _params(sem):
    return pltpu.CompilerParams(dimension_semantics=sem, vmem_limit_bytes=VMEM_LIMIT)


def _mod_call(c_all, w, b):
    n_layers, d, n = w.shape
    tn = 1536 if n % 1536 == 0 else 1024
    rows = c_all.shape[0]
    return pl.pallas_call(
        _mod_kernel,
        grid=(n_layers, n // tn),
        in_specs=[pl.BlockSpec((rows, d), lambda l, j: (0, 0)),
                  pl.BlockSpec((1, d, tn), lambda l, j: (l, 0, j)),
                  pl.BlockSpec((1, 1, tn), lambda l, j: (l, 0, j))],
        out_specs=pl.BlockSpec((1, rows, tn), lambda l, j: (l, 0, j)),
        out_shape=jax.ShapeDtypeStruct((n_layers, rows, n), F32),
        compiler_params=_params(("arbitrary", "arbitrary")),
        name="mod_proj",
    )(c_all, w, b.reshape(n_layers, 1, n))


def _rglru_call(x, mods, hist0, h0, w, bb, tt):
    b, t, d = x.shape
    c = D_RNN
    r = bb * tt
    sh, sc, gt = mods
    scratch_rows = r if tt != SUBLANES else SUBLANES
    return pl.pallas_call(
        _rglru_kernel,
        grid=(b // bb, t // tt),
        in_specs=[_tok_spec(bb, tt, d), _resident((1, d)),
                  _row_spec(bb, 1, d), _row_spec(bb, 1, d), _row_spec(bb, 1, d),
                  _row_spec(bb, SUBLANES, c), _row_spec(bb, 1, c),
                  _resident((d, 2 * c)), _resident(w["cw"].shape), _resident((1, c)),
                  _resident((2, RG_HALF, RG_HALF)), _resident((2, RG_HALF, RG_HALF)),
                  _resident((1, c)), _resident((1, c)), _resident((1, c)), _resident((c, d))],
        out_specs=[_tok_spec(bb, tt, d), _row_spec(bb, SUBLANES, c), _row_spec(bb, 1, c)],
        out_shape=[jax.ShapeDtypeStruct((b, t, d), F32),
                   jax.ShapeDtypeStruct((b, SUBLANES, c), F32),
                   jax.ShapeDtypeStruct((b, 1, c), F32)],
        compiler_params=_params(("arbitrary", "arbitrary")),
        name="rglru_mixer",
    )(x, w["g"], sh, sc, gt, hist0, h0, w["win"], w["cw"], w["cb"], w["wa"], w["wx"],
      w["ba"], w["bx"], w["lam"], w["wout"])


def _ffn_call(x, mods, hist0, w, bb, tt, final_g=None):
    b, t, d = x.shape
    sh, sc, gt = mods
    final = final_g is not None
    in_specs = [_tok_spec(bb, tt, d), _resident((1, d)),
                _row_spec(bb, 1, d), _row_spec(bb, 1, d), _row_spec(bb, 1, d),
                _row_spec(bb, SUBLANES, 2 * D_FF),
                _resident((d, 2 * D_FF)), _resident(w["cw"].shape), _resident((1, 2 * D_FF)),
                _resident((D_FF, d))]
    args = [x, w["g"], sh, sc, gt, hist0, w["wup"], w["cw"], w["cb"], w["wdn"]]
    if final:
        in_specs.append(_resident((1, d)))
        args.append(final_g)
    return pl.pallas_call(
        functools.partial(_ffn_kernel, final=final),
        grid=(b // bb, t // tt),
        in_specs=in_specs,
        out_specs=[_tok_spec(bb, tt, d), _row_spec(bb, SUBLANES, 2 * D_FF)],
        out_shape=[jax.ShapeDtypeStruct((b, t, d), F32),
                   jax.ShapeDtypeStruct((b, SUBLANES, 2 * D_FF), F32)],
        compiler_params=_params(("arbitrary", "arbitrary")),
        name="conv_ffn",
    )(*args)


def _kv_call(x, g, shift, scale, w, cum0, bb, tt, prompt):
    b, t, d = x.shape
    in_specs = [_tok_spec(bb, tt, d), _resident((1, d)), _row_spec(bb, 1, d), _row_spec(bb, 1, d),
                _resident((d, HW)), _resident((d, HW)), _resident((d, LANES)), _resident((1, LANES)),
                _row_spec(bb, 1, LANES)]
    out_specs = [_tok_spec(bb, tt, HW), _tok_spec(bb, tt, HW), _tok_spec(bb, tt, LANES)]
    out_shape = [jax.ShapeDtypeStruct((b, t, HW), F32), jax.ShapeDtypeStruct((b, t, HW), F32),
                 jax.ShapeDtypeStruct((b, t, LANES), F32)]
    args = [x, g, shift, scale, w["wk"], w["wv"], w["wf"], w["bf"], cum0]
    if prompt:
        in_specs.append(_resident(w["sel"].shape))
        args.append(w["sel"])
        out_specs += [_tok_spec(1, tt, 2 * HW), pl.BlockSpec((1, HW, tt), lambda i, j: (i, 0, j)),
                      pl.BlockSpec((1, N_HEADS, tt), lambda i, j: (i, 0, j))]
        out_shape += [jax.ShapeDtypeStruct((b, t, 2 * HW), BF16), jax.ShapeDtypeStruct((b, HW, t), BF16),
                      jax.ShapeDtypeStruct((b, N_HEADS, t), F32)]
    else:
        out_specs.append(_tok_spec(bb, tt, LANES))
        out_shape.append(jax.ShapeDtypeStruct((b, t, LANES), F32))
    return pl.pallas_call(
        functools.partial(_kv_kernel, prompt=prompt),
        grid=(b // bb, t // tt),
        in_specs=in_specs, out_specs=out_specs, out_shape=out_shape,
        scratch_shapes=[pltpu.VMEM((bb, 1, LANES), F32)],
        compiler_params=_params(("arbitrary", "arbitrary")),
        name="kv_proj",
    )(*args)


def _qproj_call(x, g, mods, wq, bb, tt, transposed):
    b, t, d = x.shape
    sh, sc, _ = mods
    if transposed:
        out_spec = pl.BlockSpec((1, HW, tt), lambda i, j: (i, 0, j))
        out_shape = jax.ShapeDtypeStruct((b, HW, t), BF16)
    else:
        out_spec = _tok_spec(bb, tt, HW)
        out_shape = jax.ShapeDtypeStruct((b, t, HW), F32)
    return pl.pallas_call(
        functools.partial(_qproj_kernel, transposed=transposed),
        grid=(b // bb, t // tt),
        in_specs=[_tok_spec(bb, tt, d), _resident((1, d)), _row_spec(bb, 1, d), _row_spec(bb, 1, d),
                  _resident((d, HW))],
        out_specs=out_spec,
        out_shape=out_shape,
        compiler_params=_params(("arbitrary", "arbitrary")),
        name="q_proj",
    )(x, g, sh, sc, wq)


def _oproj_call(x, o, gt, wo, bb, tt, transposed):
    b, t, d = x.shape
    o_spec = pl.BlockSpec((1, HW, tt), lambda i, j: (i, 0, j)) if transposed else _tok_spec(bb, tt, HW)
    return pl.pallas_call(
        functools.partial(_oproj_kernel, transposed=transposed),
        grid=(b // bb, t // tt),
        in_specs=[_tok_spec(bb, tt, d), o_spec, _row_spec(bb, 1, d), _resident((HW, d))],
        out_specs=_tok_spec(bb, tt, d),
        out_shape=jax.ShapeDtypeStruct((b, t, d), F32),
        compiler_params=_params(("arbitrary", "arbitrary")),
        name="o_proj",
    )(x, o, gt, wo)


def _flash_call(kaug, qt, vt, cumt):
    b, hw, t = qt.shape
    tq = FLASH_BLOCK
    nq = t // tq
    hp = FLASH_HEADS
    pairs = [(qi, ki) for qi in range(nq) for ki in range(qi + 1)]
    qi_tab = jnp.asarray([p[0] for p in pairs], jnp.int32)
    ki_tab = jnp.asarray([p[1] for p in pairs], jnp.int32)
    return pl.pallas_call(
        _flash_kernel,
        grid_spec=pltpu.PrefetchScalarGridSpec(
            num_scalar_prefetch=2,
            grid=(b, N_HEADS // hp, len(pairs)),
            in_specs=[pl.BlockSpec((1, tq, hp * LANES), lambda i, g, s, qi, ki: (i, ki[s], g)),
                      pl.BlockSpec((1, hp * HEAD_DIM, tq), lambda i, g, s, qi, ki: (i, g, qi[s])),
                      pl.BlockSpec((1, hp * HEAD_DIM, tq), lambda i, g, s, qi, ki: (i, g, ki[s])),
                      pl.BlockSpec((1, N_HEADS, tq), lambda i, g, s, qi, ki: (i, 0, qi[s]))],
            out_specs=pl.BlockSpec((1, hp * HEAD_DIM, tq), lambda i, g, s, qi, ki: (i, g, qi[s])),
            scratch_shapes=[pltpu.VMEM((hp, 2 * LANES, tq), BF16), pltpu.VMEM((hp, 1, tq), F32),
                            pltpu.VMEM((hp, HEAD_DIM + ONES_ROWS, tq), F32)]),
        out_shape=jax.ShapeDtypeStruct((b, hw, t), BF16),
        compiler_params=_params(("arbitrary", "arbitrary", "arbitrary")),
        name="fox_prefill",
    )(qi_tab, ki_tab, kaug, qt, vt, cumt)


def _page_spec(rows, n, per_step):
    return pl.BlockSpec((1, rows, PAGE_SIZE), lambda i, j, pt: (pt[i, j * per_step + n], 0, 0))


def _paged_cum_call(page_table, lft):
    b, n_pages = page_table.shape
    return pl.pallas_call(
        _paged_cum_kernel,
        grid_spec=pltpu.PrefetchScalarGridSpec(
            num_scalar_prefetch=1,
            grid=(b, 1),
            in_specs=[_page_spec(N_HEADS, n, n_pages) for n in range(n_pages)],
            out_specs=[pl.BlockSpec((1, N_HEADS, n_pages * PAGE_SIZE), lambda i, j, pt: (i, 0, 0)),
                       pl.BlockSpec((1, N_HEADS, PAGE_SIZE), lambda i, j, pt: (i, 0, 0))]),
        out_shape=[jax.ShapeDtypeStruct((b, N_HEADS, n_pages * PAGE_SIZE), F32),
                   jax.ShapeDtypeStruct((b, N_HEADS, PAGE_SIZE), F32)],
        compiler_params=_params(("arbitrary", "arbitrary")),
        name="paged_logf_cumsum",
    )(page_table, *([lft] * n_pages))


def _decode_attn_call(page_table, q, ktp, vtp, ckt, cq_col, k_new, v_new, ck_new):
    b, nt, _ = q.shape
    n_pages = page_table.shape[1]
    pps = DECODE_PAGES
    rows = nt * N_HEADS
    fixed = lambda shape: pl.BlockSpec(shape, lambda i, j, pt: (i, 0, 0))
    return pl.pallas_call(
        functools.partial(_decode_attn_kernel, pages_per_step=pps),
        grid_spec=pltpu.PrefetchScalarGridSpec(
            num_scalar_prefetch=1,
            grid=(b, n_pages // pps),
            in_specs=([fixed((1, nt, HW))]
                      + [_page_spec(HW, n, pps) for n in range(pps)] * 2
                      + [pl.BlockSpec((1, N_HEADS, pps * PAGE_SIZE), lambda i, j, pt: (i, 0, j)),
                         fixed((1, rows, 1)), fixed((1, nt, HW)), fixed((1, nt, HW)),
                         fixed((1, rows, PAGE_SIZE))]),
            out_specs=fixed((1, nt, HW)),
            scratch_shapes=[pltpu.VMEM((rows, HW), BF16), pltpu.VMEM((rows, 1), F32),
                            pltpu.VMEM((rows, 1), F32), pltpu.VMEM((rows, HW), F32)]),
        out_shape=jax.ShapeDtypeStruct((b, nt, HW), F32),
        compiler_params=_params(("arbitrary", "arbitrary")),
        name="fox_decode",
    )(page_table, q, *([ktp] * pps), *([vtp] * pps), ckt, cq_col, k_new, v_new, ck_new)


def _block_diag_halves(w):
    per = RG_BLOCKS // 2
    eye = jnp.eye(per, dtype=w.dtype)
    w2 = w.reshape(2, per, RG_BLOCK_W, RG_BLOCK_W)
    dense = w2[:, :, :, None, :] * eye[None, :, None, :, None]
    return dense.reshape(2, RG_HALF, RG_HALF)


def _prep_weights(p):
    row = lambda a: a.reshape(1, -1)
    layers = []
    for l in range(DEPTH):
        w = {"mix_g": row(p["norm_mix_g"][l]),
             "ffn": {"g": row(p["norm_ffn_g"][l]), "wup": p["ffn_w_up"][l].astype(BF16),
                     "cw": p["ffn_conv_w"][l], "cb": row(p["ffn_conv_b"][l]),
                     "wdn": p["ffn_w_down"][l].astype(BF16)}}
        if l < N_A_LAYERS:
            w["rg"] = {"g": w["mix_g"], "win": p["rg_w_in"][l].astype(BF16), "cw": p["rg_conv_w"][l],
                       "cb": row(p["rg_conv_b"][l]),
                       "wa": _block_diag_halves(p["rg_wa"][l]).astype(BF16),
                       "wx": _block_diag_halves(p["rg_wx"][l]).astype(BF16),
                       "ba": row(p["rg_ba"][l]), "bx": row(p["rg_bx"][l]), "lam": row(p["rg_lambda"][l]),
                       "wout": p["rg_w_out"][l].astype(BF16)}
        else:
            j = l - N_A_LAYERS
            w["wq"] = p["fa_wq"][j].astype(BF16)
            w["wo"] = p["fa_wo"][j].astype(BF16)
        layers.append(w)
    kvw = p["kv_w"]
    kv = {"g": row(p["kv_norm_g"]), "wk": kvw[:, :HW].astype(BF16), "wv": kvw[:, HW:2 * HW].astype(BF16),
          "wf": jnp.pad(kvw[:, 2 * HW:], ((0, 0), (0, LANES - N_HEADS))).astype(BF16),
          "bf": jnp.pad(p["kv_b_f"], (0, LANES - N_HEADS)).reshape(1, LANES),
          "sel": _cum_selector()}
    return layers, kv


def _cum_selector():
    piece = jnp.arange(N_SPLIT * LANES) // LANES
    head = jnp.arange(N_SPLIT * LANES) % LANES
    target = (head // 2) * LANES + N_SPLIT * (head % 2) + piece
    hit = (jnp.arange(N_HEADS // 2 * LANES)[None, :] == target[:, None]) & (head[:, None] < N_HEADS)
    return jnp.where(hit, -1.0, 0.0).astype(BF16)


def _trunk(x, mod, kv_mod, rg_conv_buf, rg_h0, ffn_buf, layers, kvw, final_g, bb, tt, past):
    bsz, t, d = x.shape
    prompt = past is None
    lead = SUBLANES - rg_conv_buf.shape[2]
    rg_hist = jnp.pad(rg_conv_buf, ((0, 0), (0, 0), (lead, 0), (0, 0)))
    lead_f = SUBLANES - ffn_buf.shape[2]
    ffn_hist = jnp.pad(ffn_buf, ((0, 0), (0, 0), (lead_f, 0), (0, 0)))
    rg_conv_new, rg_h_new, ffn_new = [], [], []
    for l in range(DEPTH):
        w = layers[l]
        m6 = [mod[l][:, i * d:(i + 1) * d].reshape(bsz, 1, d) for i in range(6)]
        mix_mods, ffn_mods = m6[:3], m6[3:]
        if l < N_A_LAYERS:
            x, tail, hl = _rglru_call(x, mix_mods, rg_hist[l], rg_h0[l].reshape(bsz, 1, D_RNN),
                                      w["rg"], bb, tt)
            rg_conv_new.append(tail[:, lead:, :])
            rg_h_new.append(hl.reshape(bsz, D_RNN))
        else:
            q = _qproj_call(x, w["mix_g"], mix_mods, w["wq"], bb, tt, prompt)
            if prompt:
                o = _flash_call(kaug, q, vt, cumt)
            else:
                o = _decode_attn_call(past["page_table"], q, past["ktp"], past["vtp"], past["ckt"],
                                      cq_col, k_new, v_new, ck_new_rows)
            x = _oproj_call(x, o, mix_mods[2], w["wo"], bb, tt, prompt)
        fin = final_g if l == DEPTH - 1 else None
        x, ftail = _ffn_call(x, ffn_mods, ffn_hist[l], w["ffn"], bb, tt, fin)
        ffn_new.append(ftail[:, lead_f:, :])
        if l == N_A_LAYERS - 1:
            kv_shift = kv_mod[:, :d].reshape(bsz, 1, d)
            kv_scale = kv_mod[:, d:].reshape(bsz, 1, d)
            if prompt:
                cum0 = jnp.zeros((bsz, 1, LANES), F32)
                k_new, v_new, lf, kaug, vt, cumt = _kv_call(
                    x, kvw["g"], kv_shift, kv_scale, kvw, cum0, bb, tt, True)
            else:
                cum0 = jnp.pad(past["total"], ((0, 0), (0, LANES - N_HEADS))).reshape(bsz, 1, LANES)
                k_new, v_new, lf, cum = _kv_call(x, kvw["g"], kv_shift, kv_scale, kvw, cum0, bb, tt, False)
                cum_new = cum[:, :, :N_HEADS]
                cq_col = cum_new.reshape(bsz, t * N_HEADS, 1)
                ck_t = jnp.transpose(cum_new, (0, 2, 1))
                ck_t = jnp.pad(ck_t, ((0, 0), (0, 0), (0, PAGE_SIZE - t)))
                ck_new_rows = jnp.tile(ck_t, (1, t, 1))
    k4 = k_new.reshape(bsz, t, N_HEADS, HEAD_DIM)
    v4 = v_new.reshape(bsz, t, N_HEADS, HEAD_DIM)
    return (x, k4, v4, lf[:, :, :N_HEADS], jnp.stack(rg_conv_new), jnp.stack(rg_h_new), jnp.stack(ffn_new))


def kernel(x_prompt, x_sample, c_prompt, c_sample, cache_k, cache_v, cache_logf, page_table,
           state_rglru_conv, state_rglru_h, state_ffn_conv, mod_w, mod_b, norm_mix_g, norm_ffn_g,
           rg_w_in, rg_conv_w, rg_conv_b, rg_wa, rg_ba, rg_wx, rg_bx, rg_lambda, rg_w_out,
           kv_norm_g, kv_mod_w, kv_mod_b, kv_w, kv_b_f, fa_wq, fa_wo,
           ffn_w_up, ffn_conv_w, ffn_conv_b, ffn_w_down, final_norm_g):
    p = dict(norm_mix_g=norm_mix_g, norm_ffn_g=norm_ffn_g, rg_w_in=rg_w_in, rg_conv_w=rg_conv_w,
             rg_conv_b=rg_conv_b, rg_wa=rg_wa, rg_ba=rg_ba, rg_wx=rg_wx, rg_bx=rg_bx,
             rg_lambda=rg_lambda, rg_w_out=rg_w_out, kv_norm_g=kv_norm_g, kv_w=kv_w, kv_b_f=kv_b_f,
             fa_wq=fa_wq, fa_wo=fa_wo, ffn_w_up=ffn_w_up, ffn_conv_w=ffn_conv_w,
             ffn_conv_b=ffn_conv_b, ffn_w_down=ffn_w_down)
    layers, kvw = _prep_weights(p)
    final_g = final_norm_g.reshape(1, D_MODEL)

    nb_p, nb_s = x_prompt.shape[0], x_sample.shape[0]
    n_all = nb_p + nb_s
    rows = -(-n_all // SUBLANES) * SUBLANES
    c_all = jnp.pad(jnp.concatenate([c_prompt, c_sample], axis=0), ((0, rows - n_all), (0, 0)))
    mod_all = _mod_call(c_all, mod_w, mod_b)
    kv_mod_all = _mod_call(c_all, kv_mod_w[None], kv_mod_b[None])[0]

    zero_rc = jnp.zeros((N_A_LAYERS, nb_p) + state_rglru_conv.shape[2:], F32)
    zero_rh = jnp.zeros((N_A_LAYERS, nb_p, D_RNN), F32)
    zero_fc = jnp.zeros((DEPTH, nb_p) + state_ffn_conv.shape[2:], F32)
    out_p = _trunk(x_prompt, mod_all[:, :nb_p], kv_mod_all[:nb_p], zero_rc, zero_rh, zero_fc,
                   layers, kvw, final_g, 1, 256, None)

    n_phys = cache_k.shape[0]
    ktp = jnp.transpose(cache_k, (0, 2, 3, 1)).reshape(n_phys, HW, PAGE_SIZE)
    vtp = jnp.transpose(cache_v, (0, 2, 3, 1)).reshape(n_phys, HW, PAGE_SIZE)
    lft = jnp.transpose(cache_logf, (0, 2, 1))
    ckt, tot = _paged_cum_call(page_table, lft)
    past = {"page_table": page_table, "ktp": ktp, "vtp": vtp, "ckt": ckt, "total": tot[:, :, 0]}
    out_s = _trunk(x_sample, mod_all[:, nb_p:n_all], kv_mod_all[nb_p:n_all], state_rglru_conv,
                   state_rglru_h, state_ffn_conv, layers, kvw, final_g, 32, SUBLANES, past)
    return (out_p[0], out_s[0]) + out_p[1:] + out_s[1:]
```

```python
import functools

import jax
import jax.numpy as jnp
from jax import lax
from jax.experimental import pallas as pl
from jax.experimental.pallas import tpu as pltpu

D_MODEL = 1024
DEPTH = 4
N_A_LAYERS = 2
D_RNN = 1280
RG_BLOCKS = 16
RG_BLOCK_W = D_RNN // RG_BLOCKS
RG_HALF = D_RNN // 2
RG_C = 8.0
N_HEADS = 16
HEAD_DIM = 64
HW = N_HEADS * HEAD_DIM
D_FF = 3072
EPS = 1e-6
PAGE_SIZE = 128
LANES = 128
SUBLANES = 8
FFN_CHUNK = 1536
N_SPLIT = 3
LOG2E = 1.4426950408889634
ONES_ROWS = 16
FLASH_BLOCK = 512
FLASH_HEADS = 16
DECODE_PAGES = 8
VMEM_LIMIT = 56 * 1024 * 1024

F32 = jnp.float32
BF16 = jnp.bfloat16


def _rmsnorm(x, g):
    ms = jnp.mean(x * x, axis=-1, keepdims=True)
    return x * lax.rsqrt(ms + EPS) * g


def _modnorm(x, g, shift, scale):
    return _rmsnorm(x, g) * (1.0 + scale) + shift


def _softplus(z):
    return jnp.maximum(z, 0.0) + jnp.log1p(jnp.exp(-jnp.abs(z)))


def _log_sigmoid(z):
    return -_softplus(-z)


def _dot(a, b):
    return jnp.dot(a, b, preferred_element_type=F32)


def _causal_conv(x3, prev8, w, b):
    width = w.shape[0]
    bb, tt, c = x3.shape
    assert tt == SUBLANES or bb == 1
    tiles = x3.reshape(bb * tt // SUBLANES, SUBLANES, c)
    t8 = lax.broadcasted_iota(jnp.int32, (1, SUBLANES, c), 1)
    y = None
    for k in range(width):
        s = width - 1 - k
        if s == 0:
            xs = tiles
        else:
            rot = pltpu.roll(tiles, s, axis=1)
            rot_prev = pltpu.roll(prev8, s, axis=1)
            if tt != SUBLANES:
                rot_prev = jnp.concatenate([rot_prev, rot[:-1]], axis=0)
            xs = jnp.where(t8 < s, rot_prev, rot)
        term = xs * w[k:k + 1][None]
        y = term if y is None else y + term
    return (y + b[None]).reshape(bb, tt, c)


def _tile_scan(a3, b3):
    shape = a3.shape
    c = shape[-1]
    a3 = a3.reshape(-1, SUBLANES, c)
    b3 = b3.reshape(-1, SUBLANES, c)
    t8 = lax.broadcasted_iota(jnp.int32, (1, SUBLANES, c), 1)
    for s in (1, 2, 4):
        keep = t8 >= s
        a_sh = jnp.where(keep, pltpu.roll(a3, s, axis=1), 1.0)
        b_sh = jnp.where(keep, pltpu.roll(b3, s, axis=1), 0.0)
        b3 = a3 * b_sh + b3
        a3 = a3 * a_sh
    return a3.reshape(shape), b3.reshape(shape)


def _mod_kernel(c_ref, w_ref, b_ref, o_ref):
    c = c_ref[...]
    act = c * jax.nn.sigmoid(c)
    o_ref[0] = _dot(act.astype(BF16), w_ref[0].astype(BF16)) + b_ref[0]


def _rglru_kernel(x_ref, g_ref, sh_ref, sc_ref, gt_ref, hist0_ref, h0_ref, win_ref, cw_ref, cb_ref,
                  wa_ref, wx_ref, ba_ref, bx_ref, lam_ref, wout_ref,
                  xo_ref, tail_ref, hlast_ref):
    bb, tt, d = x_ref.shape
    c = D_RNN
    r = bb * tt
    hw = RG_HALF

    @pl.when(pl.program_id(1) == 0)
    def _():
        tail_ref[...] = hist0_ref[...]
        hlast_ref[...] = h0_ref[...]

    x = x_ref[...]
    hnb = _modnorm(x, g_ref[...][None], sh_ref[...], sc_ref[...]).reshape(r, d).astype(BF16)

    def project(h):
        return (_dot(hnb, win_ref[0, :, h * hw:(h + 1) * hw]),
                _dot(hnb, win_ref[0, :, c + h * hw:c + (h + 1) * hw]))

    def mix(h, gate, xb):
        cols = slice(h * hw, (h + 1) * hw)
        xb3 = xb.reshape(bb, tt, hw)
        xc3 = _causal_conv(xb3, tail_ref[:, :, cols], cw_ref[:, cols], cb_ref[:, cols])
        tail_ref[:, :, cols] = xb3[:, tt - SUBLANES:, :]
        xc = xc3.reshape(r, hw)
        xcb = xc.astype(BF16)
        rg = jax.nn.sigmoid(_dot(xcb, wa_ref[h]) + ba_ref[:, cols])
        ig = jax.nn.sigmoid(_dot(xcb, wx_ref[h]) + bx_ref[:, cols])
        log_a = (-RG_C * rg) * _softplus(-lam_ref[:, cols])
        a = jnp.exp(log_a)
        y = 1.0 - a * a
        root = jnp.where(y > 0.0, y * lax.rsqrt(y), 0.0)
        bv = root * (ig * xc)
        a3, b3 = _tile_scan(a.reshape(bb, tt, hw), bv.reshape(bb, tt, hw))
        if tt == SUBLANES:
            hs3 = a3 * hlast_ref[:, :, cols] + b3
            hlast_ref[:, :, cols] = hs3[:, tt - 1:tt, :]
            hs = hs3.reshape(r, hw)
        else:
            hrow = hlast_ref[0, :, cols]
            tiles = []
            for j in range(r // SUBLANES):
                rows = slice(j * SUBLANES, (j + 1) * SUBLANES)
                hs_t = a3[0, rows] * hrow + b3[0, rows]
                tiles.append(hs_t)
                hrow = hs_t[SUBLANES - 1:SUBLANES, :]
            hlast_ref[0, :, cols] = hrow
            hs = jnp.concatenate(tiles, axis=0)
        z = (jax.nn.gelu(gate) * hs).astype(BF16)
        return _dot(z, wout_ref[0, cols, :])

    halves = [project(0), project(1)]
    y = mix(0, *halves[0]) + mix(1, *halves[1])
    xo_ref[...] = x + (1.0 + gt_ref[...]) * y.reshape(bb, tt, d)


def _ffn_kernel(*refs, final):
    if final:
        (x_ref, g_ref, sh_ref, sc_ref, gt_ref, hist0_ref, wup_ref, cw_ref, cb_ref, wdn_ref, gf_ref,
         xo_ref, tail_ref) = refs
    else:
        (x_ref, g_ref, sh_ref, sc_ref, gt_ref, hist0_ref, wup_ref, cw_ref, cb_ref, wdn_ref,
         xo_ref, tail_ref) = refs
    bb, tt, d = x_ref.shape
    r = bb * tt

    @pl.when(pl.program_id(1) == 0)
    def _():
        tail_ref[...] = hist0_ref[...]

    x = x_ref[...]
    hn = _modnorm(x, g_ref[...][None], sh_ref[...], sc_ref[...]).reshape(r, d).astype(BF16)
    n_chunks = D_FF // FFN_CHUNK

    def up(n):
        return [_dot(hn, wup_ref[0, :, lo:lo + FFN_CHUNK]).reshape(bb, tt, FFN_CHUNK)
                for lo in (n * FFN_CHUNK, D_FF + n * FFN_CHUNK)]

    acc = jnp.zeros((r, d), F32)
    u_next = up(0)
    for n in range(n_chunks):
        u_cur = u_next
        if n + 1 < n_chunks:
            u_next = up(n + 1)
        parts = []
        for u3, lo in zip(u_cur, (n * FFN_CHUNK, D_FF + n * FFN_CHUNK)):
            cols = slice(lo, lo + FFN_CHUNK)
            parts.append(_causal_conv(u3, tail_ref[:, :, cols], cw_ref[:, cols], cb_ref[:, cols]))
            tail_ref[:, :, cols] = u3[:, tt - SUBLANES:, :]
        z = (jax.nn.gelu(parts[0]) * parts[1]).reshape(r, FFN_CHUNK).astype(BF16)
        acc = acc + _dot(z, wdn_ref[0, n * FFN_CHUNK:(n + 1) * FFN_CHUNK, :])
    xn = x + (1.0 + gt_ref[...]) * acc.reshape(bb, tt, d)
    if final:
        xn = _rmsnorm(xn, gf_ref[...][None])
    xo_ref[...] = xn


def _split3(x):
    hi = x.astype(BF16).astype(F32)
    r1 = x - hi
    mid = r1.astype(BF16).astype(F32)
    lo = (r1 - mid).astype(BF16).astype(F32)
    return hi, mid, lo


def _kv_kernel(*refs, prompt):
    if prompt:
        (x_ref, g_ref, sh_ref, sc_ref, wk_ref, wv_ref, wf_ref, bf_ref, cum0_ref, sel_ref,
         k_ref, v_ref, lf_ref, kaug_ref, vt_ref, cumt_ref, carry) = refs
    else:
        (x_ref, g_ref, sh_ref, sc_ref, wk_ref, wv_ref, wf_ref, bf_ref, cum0_ref,
         k_ref, v_ref, lf_ref, cum_ref, carry) = refs
    bb, tt, d = x_ref.shape
    r = bb * tt

    @pl.when(pl.program_id(1) == 0)
    def _():
        carry[...] = cum0_ref[...]

    hk = _modnorm(x_ref[...], g_ref[...][None], sh_ref[...], sc_ref[...]).reshape(r, d).astype(BF16)
    k = _dot(hk, wk_ref[...])
    v = _dot(hk, wv_ref[...])
    lf3 = _log_sigmoid(_dot(hk, wf_ref[...]) + bf_ref[...]).reshape(bb, tt, LANES)
    k_ref[...] = k.reshape(bb, tt, HW)
    v_ref[...] = v.reshape(bb, tt, HW)
    lf_ref[...] = lf3
    tpos = lax.broadcasted_iota(jnp.int32, (1, tt, LANES), 1)
    cs = lf3
    s = 1
    while s < tt:
        cs = cs + jnp.where(tpos >= s, pltpu.roll(cs, s, axis=1), 0.0)
        s *= 2
    cum = cs + carry[...]
    carry[...] = cum[:, tt - 1:tt, :]
    if not prompt:
        cum_ref[...] = cum
        return
    ck = cum[0] * LOG2E
    pieces = jnp.concatenate(_split3(ck), axis=1).astype(BF16)
    kx = _dot(pieces, sel_ref[...])
    lane = lax.broadcasted_iota(jnp.int32, (1, kx.shape[1]), 1) % LANES
    kx = jnp.where((lane >= 2 * N_SPLIT) & (lane < 3 * N_SPLIT), 1.0, kx).astype(BF16)
    kb = k.astype(BF16)
    for p in range(N_HEADS // 2):
        kaug_ref[0, :, 2 * p * LANES:(2 * p + 1) * LANES] = kb[:, p * LANES:(p + 1) * LANES]
        kaug_ref[0, :, (2 * p + 1) * LANES:(2 * p + 2) * LANES] = kx[:, p * LANES:(p + 1) * LANES]
    vt_ref[0] = v.T.astype(BF16)
    cumt_ref[0] = ck.T[:N_HEADS, :]


def _qproj_kernel(x_ref, g_ref, sh_ref, sc_ref, wq_ref, q_ref, *, transposed):
    bb, tt, d = x_ref.shape
    hn = _modnorm(x_ref[...], g_ref[...][None], sh_ref[...], sc_ref[...])
    q = _dot(hn.reshape(bb * tt, d).astype(BF16), wq_ref[0]) * (HEAD_DIM ** -0.5)
    if transposed:
        q_ref[0] = (q * LOG2E).T.astype(q_ref.dtype)
    else:
        q_ref[...] = q.reshape(bb, tt, HW).astype(q_ref.dtype)


def _oproj_kernel(x_ref, o_ref, gt_ref, wo_ref, xo_ref, *, transposed):
    bb, tt, d = x_ref.shape
    if transposed:
        y = lax.dot_general(o_ref[0], wo_ref[0], (((0,), (0,)), ((), ())), preferred_element_type=F32)
    else:
        y = _dot(o_ref[...].reshape(bb * tt, HW).astype(BF16), wo_ref[0])
    xo_ref[...] = x_ref[...] + (1.0 + gt_ref[...]) * y.reshape(bb, tt, d)


def _flash_kernel(qi_ref, ki_ref, kaug_ref, qt_ref, vt_ref, cumt_ref, o_ref, qa_s, m_s, acc_s):
    group = pl.program_id(1)
    step_id = pl.program_id(2)
    qi = qi_ref[step_id]
    ki = ki_ref[step_id]
    n_hp, _, tq = qa_s.shape
    tk = kaug_ref.shape[1]
    row = lax.broadcasted_iota(jnp.int32, (LANES, 1), 0)

    @pl.when(ki == 0)
    def _():
        m_s[...] = jnp.full(m_s.shape, -jnp.inf, F32)
        acc_s[...] = jnp.zeros(acc_s.shape, F32)
        for hh in range(n_hp):
            pr, j = divmod(hh, 2)
            qt = qt_ref[0, pr * LANES:(pr + 1) * LANES, :]
            in_head = (row >= j * HEAD_DIM) & (row < (j + 1) * HEAD_DIM)
            qa_s[hh, :LANES, :] = jnp.where(in_head, qt, jnp.zeros_like(qt))
            hi, mid, lo = _split3(cumt_ref[0, pl.ds(group * n_hp + hh, 1), :])
            pick = ((row >= j * N_SPLIT) & (row < (j + 1) * N_SPLIT)).astype(F32)
            ex = jnp.where(row == 2 * N_SPLIT, hi,
                           jnp.where(row == 2 * N_SPLIT + 1, mid,
                                     jnp.where(row == 2 * N_SPLIT + 2, lo, pick)))
            qa_s[hh, LANES:, :] = ex.astype(BF16)

    def step(masked):
        scores = []
        for hh in range(n_hp):
            pr, j = divmod(hh, 2)
            ka = kaug_ref[0, :, 2 * pr * LANES:(2 * pr + 2) * LANES]
            s = _dot(ka, qa_s[hh])
            if masked:
                kpos = lax.broadcasted_iota(jnp.int32, (tk, tq), 0)
                qpos = lax.broadcasted_iota(jnp.int32, (tk, tq), 1)
                s = jnp.where(kpos <= qpos, s, -jnp.inf)
            scores.append(s)
        probs = []
        for hh in range(n_hp):
            s = scores[hh]
            m_prev = m_s[hh]
            m_new = jnp.maximum(m_prev, jnp.max(s, axis=0, keepdims=True))
            probs.append((jnp.exp2(m_prev - m_new), jnp.exp2(s - m_new).astype(BF16)))
            m_s[hh] = m_new
        ones = jnp.ones((acc_s.shape[1] - HEAD_DIM, tk), BF16)
        for hh in range(n_hp):
            alpha, p = probs[hh]
            vta = jnp.concatenate([vt_ref[0, hh * HEAD_DIM:(hh + 1) * HEAD_DIM, :], ones], axis=0)
            acc_s[hh] = alpha * acc_s[hh] + _dot(vta, p)

    @pl.when(ki < qi)
    def _():
        step(False)

    @pl.when(ki == qi)
    def _():
        step(True)
        for hh in range(n_hp):
            acc = acc_s[hh]
            o = acc[:HEAD_DIM] / acc[HEAD_DIM:HEAD_DIM + 1]
            o_ref[0, hh * HEAD_DIM:(hh + 1) * HEAD_DIM, :] = o.astype(o_ref.dtype)


def _paged_cum_kernel(pt_ref, *refs):
    n_pages = len(refs) - 2
    ck_ref, tot_ref = refs[n_pages:]
    cs = jnp.concatenate([refs[n][0] for n in range(n_pages)], axis=0)
    lane = lax.broadcasted_iota(jnp.int32, cs.shape, 1)
    s = 1
    while s < PAGE_SIZE:
        cs = cs + jnp.where(lane >= s, pltpu.roll(cs, s, axis=1), 0.0)
        s *= 2
    totals = jnp.broadcast_to(cs[:, PAGE_SIZE - 1:PAGE_SIZE], cs.shape)
    carry = jnp.zeros((N_HEADS, PAGE_SIZE), F32)
    for n in range(n_pages):
        rows = slice(n * N_HEADS, (n + 1) * N_HEADS)
        ck_ref[0, :, n * PAGE_SIZE:(n + 1) * PAGE_SIZE] = cs[rows] + carry
        carry = carry + totals[rows]
    tot_ref[0] = carry


def _decode_attn_kernel(pt_ref, *refs, pages_per_step):
    pps = pages_per_step
    q_ref = refs[0]
    kt_refs = refs[1:1 + pps]
    vt_refs = refs[1 + pps:1 + 2 * pps]
    ck_ref, cq_ref, kn_ref, vn_ref, ckn_ref, o_ref, qbd_s, m_s, l_s, acc_s = refs[1 + 2 * pps:]
    j = pl.program_id(1)
    rows = qbd_s.shape[0]
    nt = rows // N_HEADS
    nk = pps * PAGE_SIZE
    head_mask = (lax.broadcasted_iota(jnp.int32, (N_HEADS, HW), 1) // HEAD_DIM
                 == lax.broadcasted_iota(jnp.int32, (N_HEADS, HW), 0))

    @pl.when(j == 0)
    def _():
        m_s[...] = jnp.full(m_s.shape, -jnp.inf, F32)
        l_s[...] = jnp.zeros(l_s.shape, F32)
        acc_s[...] = jnp.zeros(acc_s.shape, F32)
        q = q_ref[0]
        qbd = jnp.where(head_mask[None], q[:, None, :], 0.0)
        qbd_s[...] = qbd.reshape(rows, HW).astype(BF16)

    def update(s, pv_fn):
        m_prev = m_s[...]
        m_new = jnp.maximum(m_prev, jnp.max(s, axis=1, keepdims=True))
        alpha = jnp.exp(m_prev - m_new)
        p = jnp.exp(s - m_new)
        l_s[...] = alpha * l_s[...] + jnp.sum(p, axis=1, keepdims=True)
        acc_s[...] = alpha * acc_s[...] + pv_fn(p.astype(BF16))
        m_s[...] = m_new

    kt = jnp.concatenate([r[0].astype(BF16) for r in kt_refs], axis=1)
    vt = jnp.concatenate([r[0].astype(BF16) for r in vt_refs], axis=1)
    s = _dot(qbd_s[...], kt)
    bias = cq_ref[0].reshape(nt, N_HEADS, 1) - ck_ref[0][None]
    s = (s.reshape(nt, N_HEADS, nk) + bias).reshape(rows, nk)
    update(s, lambda p: lax.dot_general(p, vt, (((1,), (1,)), ((), ())), preferred_element_type=F32))

    @pl.when(j == pl.num_programs(1) - 1)
    def _():
        pad = jnp.zeros((PAGE_SIZE - nt, HW), F32)
        kb = jnp.concatenate([kn_ref[0], pad], axis=0).astype(BF16)
        vb = jnp.concatenate([vn_ref[0], pad], axis=0).astype(BF16)
        tok = lax.broadcasted_iota(jnp.int32, (rows, PAGE_SIZE), 0) // N_HEADS
        key = lax.broadcasted_iota(jnp.int32, (rows, PAGE_SIZE), 1)
        sn = lax.dot_general(qbd_s[...], kb, (((1,), (1,)), ((), ())), preferred_element_type=F32)
        sn = jnp.where(key <= tok, sn + cq_ref[0] - ckn_ref[0], -jnp.inf)
        update(sn, lambda p: _dot(p, vb))
        o_full = (acc_s[...] / l_s[...]).reshape(nt, N_HEADS, HW)
        o_ref[0] = jnp.sum(jnp.where(head_mask[None], o_full, 0.0), axis=1)


def _resident(shape):
    nd = len(shape)
    return pl.BlockSpec(shape, lambda *_: (0,) * nd, pipeline_mode=pl.Buffered(1))


def _layer_spec(stacked):
    stack, layer = stacked
    return pl.BlockSpec((1,) + stack.shape[1:], lambda *_: (layer, 0, 0), pipeline_mode=pl.Buffered(1))


def _tok_spec(bb, tt, width):
    return pl.BlockSpec((bb, tt, width), lambda i, t: (i, t, 0))


def _row_spec(bb, rows, width):
    return pl.BlockSpec((bb, rows, width), lambda i, t: (i, 0, 0))


def _params(sem):
    return pltpu.CompilerParams(dimension_semantics=sem, vmem_limit_bytes=VMEM_LIMIT)


def _mod_call(c_all, w, b):
    n_layers, d, n = w.shape
    tn = 1536 if n % 1536 == 0 else 1024
    rows = c_all.shape[0]
    return pl.pallas_call(
        _mod_kernel,
        grid=(n_layers, n // tn),
        in_specs=[pl.BlockSpec((rows, d), lambda l, j: (0, 0)),
                  pl.BlockSpec((1, d, tn), lambda l, j: (l, 0, j)),
                  pl.BlockSpec((1, 1, tn), lambda l, j: (l, 0, j))],
        out_specs=pl.BlockSpec((1, rows, tn), lambda l, j: (l, 0, j)),
        out_shape=jax.ShapeDtypeStruct((n_layers, rows, n), F32),
        compiler_params=_params(("arbitrary", "arbitrary")),
        name="mod_proj",
    )(c_all, w, b.reshape(n_layers, 1, n))


def _rglru_call(x, mods, hist0, h0, w, bb, tt):
    b, t, d = x.shape
    c = D_RNN
    r = bb * tt
    sh, sc, gt = mods
    scratch_rows = r if tt != SUBLANES else SUBLANES
    return pl.pallas_call(
        _rglru_kernel,
        grid=(b // bb, t // tt),
        in_specs=[_tok_spec(bb, tt, d), _resident((1, d)),
                  _row_spec(bb, 1, d), _row_spec(bb, 1, d), _row_spec(bb, 1, d),
                  _row_spec(bb, SUBLANES, c), _row_spec(bb, 1, c),
                  _layer_spec(w["win"]), _resident(w["cw"].shape), _resident((1, c)),
                  _resident((2, RG_HALF, RG_HALF)), _resident((2, RG_HALF, RG_HALF)),
                  _resident((1, c)), _resident((1, c)), _resident((1, c)), _layer_spec(w["wout"])],
        out_specs=[_tok_spec(bb, tt, d), _row_spec(bb, SUBLANES, c), _row_spec(bb, 1, c)],
        out_shape=[jax.ShapeDtypeStruct((b, t, d), F32),
                   jax.ShapeDtypeStruct((b, SUBLANES, c), F32),
                   jax.ShapeDtypeStruct((b, 1, c), F32)],
        compiler_params=_params(("arbitrary", "arbitrary")),
        name="rglru_mixer",
    )(x, w["g"], sh, sc, gt, hist0, h0, w["win"][0], w["cw"], w["cb"], w["wa"], w["wx"],
      w["ba"], w["bx"], w["lam"], w["wout"][0])


def _ffn_call(x, mods, hist0, w, bb, tt, final_g=None):
    b, t, d = x.shape
    sh, sc, gt = mods
    final = final_g is not None
    in_specs = [_tok_spec(bb, tt, d), _resident((1, d)),
                _row_spec(bb, 1, d), _row_spec(bb, 1, d), _row_spec(bb, 1, d),
                _row_spec(bb, SUBLANES, 2 * D_FF),
                _layer_spec(w["wup"]), _resident(w["cw"].shape), _resident((1, 2 * D_FF)),
                _layer_spec(w["wdn"])]
    args = [x, w["g"], sh, sc, gt, hist0, w["wup"][0], w["cw"], w["cb"], w["wdn"][0]]
    if final:
        in_specs.append(_resident((1, d)))
        args.append(final_g)
    return pl.pallas_call(
        functools.partial(_ffn_kernel, final=final),
        grid=(b // bb, t // tt),
        in_specs=in_specs,
        out_specs=[_tok_spec(bb, tt, d), _row_spec(bb, SUBLANES, 2 * D_FF)],
        out_shape=[jax.ShapeDtypeStruct((b, t, d), F32),
                   jax.ShapeDtypeStruct((b, SUBLANES, 2 * D_FF), F32)],
        compiler_params=_params(("arbitrary", "arbitrary")),
        name="conv_ffn",
    )(*args)


def _kv_call(x, g, shift, scale, w, cum0, bb, tt, prompt):
    b, t, d = x.shape
    in_specs = [_tok_spec(bb, tt, d), _resident((1, d)), _row_spec(bb, 1, d), _row_spec(bb, 1, d),
                _resident((d, HW)), _resident((d, HW)), _resident((d, LANES)), _resident((1, LANES)),
                _row_spec(bb, 1, LANES)]
    out_specs = [_tok_spec(bb, tt, HW), _tok_spec(bb, tt, HW), _tok_spec(bb, tt, LANES)]
    out_shape = [jax.ShapeDtypeStruct((b, t, HW), F32), jax.ShapeDtypeStruct((b, t, HW), F32),
                 jax.ShapeDtypeStruct((b, t, LANES), F32)]
    args = [x, g, shift, scale, w["wk"], w["wv"], w["wf"], w["bf"], cum0]
    if prompt:
        in_specs.append(_resident(w["sel"].shape))
        args.append(w["sel"])
        out_specs += [_tok_spec(1, tt, 2 * HW), pl.BlockSpec((1, HW, tt), lambda i, j: (i, 0, j)),
                      pl.BlockSpec((1, N_HEADS, tt), lambda i, j: (i, 0, j))]
        out_shape += [jax.ShapeDtypeStruct((b, t, 2 * HW), BF16), jax.ShapeDtypeStruct((b, HW, t), BF16),
                      jax.ShapeDtypeStruct((b, N_HEADS, t), F32)]
    else:
        out_specs.append(_tok_spec(bb, tt, LANES))
        out_shape.append(jax.ShapeDtypeStruct((b, t, LANES), F32))
    return pl.pallas_call(
        functools.partial(_kv_kernel, prompt=prompt),
        grid=(b // bb, t // tt),
        in_specs=in_specs, out_specs=out_specs, out_shape=out_shape,
        scratch_shapes=[pltpu.VMEM((bb, 1, LANES), F32)],
        compiler_params=_params(("arbitrary", "arbitrary")),
        name="kv_proj",
    )(*args)


def _qproj_call(x, g, mods, wq, bb, tt, transposed):
    b, t, d = x.shape
    sh, sc, _ = mods
    if transposed:
        out_spec = pl.BlockSpec((1, HW, tt), lambda i, j: (i, 0, j))
        out_shape = jax.ShapeDtypeStruct((b, HW, t), BF16)
    else:
        out_spec = _tok_spec(bb, tt, HW)
        out_shape = jax.ShapeDtypeStruct((b, t, HW), F32)
    return pl.pallas_call(
        functools.partial(_qproj_kernel, transposed=transposed),
        grid=(b // bb, t // tt),
        in_specs=[_tok_spec(bb, tt, d), _resident((1, d)), _row_spec(bb, 1, d), _row_spec(bb, 1, d),
                  _layer_spec(wq)],
        out_specs=out_spec,
        out_shape=out_shape,
        compiler_params=_params(("arbitrary", "arbitrary")),
        name="q_proj",
    )(x, g, sh, sc, wq[0])


def _oproj_call(x, o, gt, wo, bb, tt, transposed):
    b, t, d = x.shape
    o_spec = pl.BlockSpec((1, HW, tt), lambda i, j: (i, 0, j)) if transposed else _tok_spec(bb, tt, HW)
    return pl.pallas_call(
        functools.partial(_oproj_kernel, transposed=transposed),
        grid=(b // bb, t // tt),
        in_specs=[_tok_spec(bb, tt, d), o_spec, _row_spec(bb, 1, d), _layer_spec(wo)],
        out_specs=_tok_spec(bb, tt, d),
        out_shape=jax.ShapeDtypeStruct((b, t, d), F32),
        compiler_params=_params(("arbitrary", "arbitrary")),
        name="o_proj",
    )(x, o, gt, wo[0])


def _flash_call(kaug, qt, vt, cumt):
    b, hw, t = qt.shape
    tq = FLASH_BLOCK
    nq = t // tq
    hp = FLASH_HEADS
    pairs = [(qi, ki) for qi in range(nq) for ki in range(qi + 1)]
    qi_tab = jnp.asarray([p[0] for p in pairs], jnp.int32)
    ki_tab = jnp.asarray([p[1] for p in pairs], jnp.int32)
    return pl.pallas_call(
        _flash_kernel,
        grid_spec=pltpu.PrefetchScalarGridSpec(
            num_scalar_prefetch=2,
            grid=(b, N_HEADS // hp, len(pairs)),
            in_specs=[pl.BlockSpec((1, tq, hp * LANES), lambda i, g, s, qi, ki: (i, ki[s], g)),
                      pl.BlockSpec((1, hp * HEAD_DIM, tq), lambda i, g, s, qi, ki: (i, g, qi[s])),
                      pl.BlockSpec((1, hp * HEAD_DIM, tq), lambda i, g, s, qi, ki: (i, g, ki[s])),
                      pl.BlockSpec((1, N_HEADS, tq), lambda i, g, s, qi, ki: (i, 0, qi[s]))],
            out_specs=pl.BlockSpec((1, hp * HEAD_DIM, tq), lambda i, g, s, qi, ki: (i, g, qi[s])),
            scratch_shapes=[pltpu.VMEM((hp, 2 * LANES, tq), BF16), pltpu.VMEM((hp, 1, tq), F32),
                            pltpu.VMEM((hp, HEAD_DIM + ONES_ROWS, tq), F32)]),
        out_shape=jax.ShapeDtypeStruct((b, hw, t), BF16),
        compiler_params=_params(("arbitrary", "arbitrary", "arbitrary")),
        name="fox_prefill",
    )(qi_tab, ki_tab, kaug, qt, vt, cumt)


def _page_spec(rows, n, per_step):
    return pl.BlockSpec((1, rows, PAGE_SIZE), lambda i, j, pt: (pt[i, j * per_step + n], 0, 0))


def _paged_cum_call(page_table, lft):
    b, n_pages = page_table.shape
    return pl.pallas_call(
        _paged_cum_kernel,
        grid_spec=pltpu.PrefetchScalarGridSpec(
            num_scalar_prefetch=1,
            grid=(b, 1),
            in_specs=[_page_spec(N_HEADS, n, n_pages) for n in range(n_pages)],
            out_specs=[pl.BlockSpec((1, N_HEADS, n_pages * PAGE_SIZE), lambda i, j, pt: (i, 0, 0)),
                       pl.BlockSpec((1, N_HEADS, PAGE_SIZE), lambda i, j, pt: (i, 0, 0))]),
        out_shape=[jax.ShapeDtypeStruct((b, N_HEADS, n_pages * PAGE_SIZE), F32),
                   jax.ShapeDtypeStruct((b, N_HEADS, PAGE_SIZE), F32)],
        compiler_params=_params(("arbitrary", "arbitrary")),
        name="paged_logf_cumsum",
    )(page_table, *([lft] * n_pages))


def _decode_attn_call(page_table, q, ktp, vtp, ckt, cq_col, k_new, v_new, ck_new):
    b, nt, _ = q.shape
    n_pages = page_table.shape[1]
    pps = DECODE_PAGES
    rows = nt * N_HEADS
    fixed = lambda shape: pl.BlockSpec(shape, lambda i, j, pt: (i, 0, 0))
    return pl.pallas_call(
        functools.partial(_decode_attn_kernel, pages_per_step=pps),
        grid_spec=pltpu.PrefetchScalarGridSpec(
            num_scalar_prefetch=1,
            grid=(b, n_pages // pps),
            in_specs=([fixed((1, nt, HW))]
                      + [_page_spec(HW, n, pps) for n in range(pps)] * 2
                      + [pl.BlockSpec((1, N_HEADS, pps * PAGE_SIZE), lambda i, j, pt: (i, 0, j)),
                         fixed((1, rows, 1)), fixed((1, nt, HW)), fixed((1, nt, HW)),
                         fixed((1, rows, PAGE_SIZE))]),
            out_specs=fixed((1, nt, HW)),
            scratch_shapes=[pltpu.VMEM((rows, HW), BF16), pltpu.VMEM((rows, 1), F32),
                            pltpu.VMEM((rows, 1), F32), pltpu.VMEM((rows, HW), F32)]),
        out_shape=jax.ShapeDtypeStruct((b, nt, HW), F32),
        compiler_params=_params(("arbitrary", "arbitrary")),
        name="fox_decode",
    )(page_table, q, *([ktp] * pps), *([vtp] * pps), ckt, cq_col, k_new, v_new, ck_new)


def _block_diag_halves(w):
    per = RG_BLOCKS // 2
    eye = jnp.eye(per, dtype=w.dtype)
    w2 = w.reshape(2, per, RG_BLOCK_W, RG_BLOCK_W)
    dense = w2[:, :, :, None, :] * eye[None, :, None, :, None]
    return dense.reshape(2, RG_HALF, RG_HALF)


def _prep_weights(p):
    row = lambda a: a.reshape(1, -1)
    stacks = {k: p[k].astype(BF16) for k in ("ffn_w_up", "ffn_w_down", "rg_w_in", "rg_w_out", "fa_wq", "fa_wo")}
    layers = []
    for l in range(DEPTH):
        w = {"mix_g": row(p["norm_mix_g"][l]),
             "ffn": {"g": row(p["norm_ffn_g"][l]), "wup": (stacks["ffn_w_up"], l),
                     "cw": p["ffn_conv_w"][l], "cb": row(p["ffn_conv_b"][l]),
                     "wdn": (stacks["ffn_w_down"], l)}}
        if l < N_A_LAYERS:
            w["rg"] = {"g": w["mix_g"], "win": (stacks["rg_w_in"], l), "cw": p["rg_conv_w"][l],
                       "cb": row(p["rg_conv_b"][l]),
                       "wa": _block_diag_halves(p["rg_wa"][l]).astype(BF16),
                       "wx": _block_diag_halves(p["rg_wx"][l]).astype(BF16),
                       "ba": row(p["rg_ba"][l]), "bx": row(p["rg_bx"][l]), "lam": row(p["rg_lambda"][l]),
                       "wout": (stacks["rg_w_out"], l)}
        else:
            j = l - N_A_LAYERS
            w["wq"] = (stacks["fa_wq"], j)
            w["wo"] = (stacks["fa_wo"], j)
        layers.append(w)
    kvw = p["kv_w"]
    kv = {"g": row(p["kv_norm_g"]), "wk": kvw[:, :HW].astype(BF16), "wv": kvw[:, HW:2 * HW].astype(BF16),
          "wf": jnp.pad(kvw[:, 2 * HW:], ((0, 0), (0, LANES - N_HEADS))).astype(BF16),
          "bf": jnp.pad(p["kv_b_f"], (0, LANES - N_HEADS)).reshape(1, LANES),
          "sel": _cum_selector()}
    return layers, kv


def _cum_selector():
    piece = jnp.arange(N_SPLIT * LANES) // LANES
    head = jnp.arange(N_SPLIT * LANES) % LANES
    target = (head // 2) * LANES + N_SPLIT * (head % 2) + piece
    hit = (jnp.arange(N_HEADS // 2 * LANES)[None, :] == target[:, None]) & (head[:, None] < N_HEADS)
    return jnp.where(hit, -1.0, 0.0).astype(BF16)


def _trunk(x, mod, kv_mod, rg_conv_buf, rg_h0, ffn_buf, layers, kvw, final_g, bb, tt, past):
    bsz, t, d = x.shape
    prompt = past is None
    lead = SUBLANES - rg_conv_buf.shape[2]
    rg_hist = jnp.pad(rg_conv_buf, ((0, 0), (0, 0), (lead, 0), (0, 0)))
    lead_f = SUBLANES - ffn_buf.shape[2]
    ffn_hist = jnp.pad(ffn_buf, ((0, 0), (0, 0), (lead_f, 0), (0, 0)))
    rg_conv_new, rg_h_new, ffn_new = [], [], []
    for l in range(DEPTH):
        w = layers[l]
        m6 = [mod[l][:, i * d:(i + 1) * d].reshape(bsz, 1, d) for i in range(6)]
        mix_mods, ffn_mods = m6[:3], m6[3:]
        if l < N_A_LAYERS:
            x, tail, hl = _rglru_call(x, mix_mods, rg_hist[l], rg_h0[l].reshape(bsz, 1, D_RNN),
                                      w["rg"], bb, tt)
            rg_conv_new.append(tail[:, lead:, :])
            rg_h_new.append(hl.reshape(bsz, D_RNN))
        else:
            q = _qproj_call(x, w["mix_g"], mix_mods, w["wq"], bb, tt, prompt)
            if prompt:
                o = _flash_call(kaug, q, vt, cumt)
            else:
                o = _decode_attn_call(past["page_table"], q, past["ktp"], past["vtp"], past["ckt"],
                                      cq_col, k_new, v_new, ck_new_rows)
            x = _oproj_call(x, o, mix_mods[2], w["wo"], bb, tt, prompt)
        fin = final_g if l == DEPTH - 1 else None
        x, ftail = _ffn_call(x, ffn_mods, ffn_hist[l], w["ffn"], bb, tt, fin)
        ffn_new.append(ftail[:, lead_f:, :])
        if l == N_A_LAYERS - 1:
            kv_shift = kv_mod[:, :d].reshape(bsz, 1, d)
            kv_scale = kv_mod[:, d:].reshape(bsz, 1, d)
            if prompt:
                cum0 = jnp.zeros((bsz, 1, LANES), F32)
                k_new, v_new, lf, kaug, vt, cumt = _kv_call(
                    x, kvw["g"], kv_shift, kv_scale, kvw, cum0, bb, tt, True)
            else:
                cum0 = jnp.pad(past["total"], ((0, 0), (0, LANES - N_HEADS))).reshape(bsz, 1, LANES)
                k_new, v_new, lf, cum = _kv_call(x, kvw["g"], kv_shift, kv_scale, kvw, cum0, bb, tt, False)
                cum_new = cum[:, :, :N_HEADS]
                cq_col = cum_new.reshape(bsz, t * N_HEADS, 1)
                ck_t = jnp.transpose(cum_new, (0, 2, 1))
                ck_t = jnp.pad(ck_t, ((0, 0), (0, 0), (0, PAGE_SIZE - t)))
                ck_new_rows = jnp.tile(ck_t, (1, t, 1))
    k4 = k_new.reshape(bsz, t, N_HEADS, HEAD_DIM)
    v4 = v_new.reshape(bsz, t, N_HEADS, HEAD_DIM)
    return (x, k4, v4, lf[:, :, :N_HEADS], jnp.stack(rg_conv_new), jnp.stack(rg_h_new), jnp.stack(ffn_new))


def kernel(x_prompt, x_sample, c_prompt, c_sample, cache_k, cache_v, cache_logf, page_table,
           state_rglru_conv, state_rglru_h, state_ffn_conv, mod_w, mod_b, norm_mix_g, norm_ffn_g,
           rg_w_in, rg_conv_w, rg_conv_b, rg_wa, rg_ba, rg_wx, rg_bx, rg_lambda, rg_w_out,
           kv_norm_g, kv_mod_w, kv_mod_b, kv_w, kv_b_f, fa_wq, fa_wo,
           ffn_w_up, ffn_conv_w, ffn_conv_b, ffn_w_down, final_norm_g):
    p = dict(norm_mix_g=norm_mix_g, norm_ffn_g=norm_ffn_g, rg_w_in=rg_w_in, rg_conv_w=rg_conv_w,
             rg_conv_b=rg_conv_b, rg_wa=rg_wa, rg_ba=rg_ba, rg_wx=rg_wx, rg_bx=rg_bx,
             rg_lambda=rg_lambda, rg_w_out=rg_w_out, kv_norm_g=kv_norm_g, kv_w=kv_w, kv_b_f=kv_b_f,
             fa_wq=fa_wq, fa_wo=fa_wo, ffn_w_up=ffn_w_up, ffn_conv_w=ffn_conv_w,
             ffn_conv_b=ffn_conv_b, ffn_w_down=ffn_w_down)
    layers, kvw = _prep_weights(p)
    final_g = final_norm_g.reshape(1, D_MODEL)

    nb_p, nb_s = x_prompt.shape[0], x_sample.shape[0]
    n_all = nb_p + nb_s
    rows = -(-n_all // SUBLANES) * SUBLANES
    c_all = jnp.pad(jnp.concatenate([c_prompt, c_sample], axis=0), ((0, rows - n_all), (0, 0)))
    mod_all = _mod_call(c_all, mod_w, mod_b)
    kv_mod_all = _mod_call(c_all, kv_mod_w[None], kv_mod_b[None])[0]

    zero_rc = jnp.zeros((N_A_LAYERS, nb_p) + state_rglru_conv.shape[2:], F32)
    zero_rh = jnp.zeros((N_A_LAYERS, nb_p, D_RNN), F32)
    zero_fc = jnp.zeros((DEPTH, nb_p) + state_ffn_conv.shape[2:], F32)
    out_p = _trunk(x_prompt, mod_all[:, :nb_p], kv_mod_all[:nb_p], zero_rc, zero_rh, zero_fc,
                   layers, kvw, final_g, 1, 512, None)

    n_phys = cache_k.shape[0]
    ktp = jnp.transpose(cache_k, (0, 2, 3, 1)).reshape(n_phys, HW, PAGE_SIZE)
    vtp = jnp.transpose(cache_v, (0, 2, 3, 1)).reshape(n_phys, HW, PAGE_SIZE)
    lft = jnp.transpose(cache_logf, (0, 2, 1))
    ckt, tot = _paged_cum_call(page_table, lft)
    past = {"page_table": page_table, "ktp": ktp, "vtp": vtp, "ckt": ckt, "total": tot[:, :, 0]}
    out_s = _trunk(x_sample, mod_all[:, nb_p:n_all], kv_mod_all[nb_p:n_all], state_rglru_conv,
                   state_rglru_h, state_ffn_conv, layers, kvw, final_g, 32, SUBLANES, past)
    return (out_p[0], out_s[0]) + out_p[1:] + out_s[1:]
```

```python
import functools

import jax
import jax.numpy as jnp
from jax import lax
from jax.experimental import pallas as pl
from jax.experimental.pallas import tpu as pltpu

D_MODEL = 1024
DEPTH = 4
N_A_LAYERS = 2
D_RNN = 1280
RG_BLOCKS = 16
RG_BLOCK_W = D_RNN // RG_BLOCKS
RG_HALF = D_RNN // 2
RG_C = 8.0
N_HEADS = 16
HEAD_DIM = 64
HW = N_HEADS * HEAD_DIM
D_FF = 3072
EPS = 1e-6
PAGE_SIZE = 128
LANES = 128
SUBLANES = 8
FFN_CHUNK = 1536
N_SPLIT = 3
LOG2E = 1.4426950408889634
ONES_ROWS = 16
FLASH_BLOCK = 512
FLASH_HEADS = 16
DECODE_PAGES = 8
VMEM_LIMIT = 56 * 1024 * 1024

F32 = jnp.float32
BF16 = jnp.bfloat16


def _rmsnorm(x, g):
    ms = jnp.mean(x * x, axis=-1, keepdims=True)
    return x * lax.rsqrt(ms + EPS) * g


def _modnorm(x, g, shift, scale):
    return _rmsnorm(x, g) * (1.0 + scale) + shift


def _softplus(z):
    return jnp.maximum(z, 0.0) + jnp.log1p(jnp.exp(-jnp.abs(z)))


def _log_sigmoid(z):
    return -_softplus(-z)


def _dot(a, b):
    return jnp.dot(a, b, preferred_element_type=F32)


def _causal_conv(x3, prev8, w, b):
    width = w.shape[0]
    bb, tt, c = x3.shape
    assert tt == SUBLANES or bb == 1
    tiles = x3.reshape(bb * tt // SUBLANES, SUBLANES, c)
    t8 = lax.broadcasted_iota(jnp.int32, (1, SUBLANES, c), 1)
    y = None
    for k in range(width):
        s = width - 1 - k
        if s == 0:
            xs = tiles
        else:
            rot = pltpu.roll(tiles, s, axis=1)
            rot_prev = pltpu.roll(prev8, s, axis=1)
            if tt != SUBLANES:
                rot_prev = jnp.concatenate([rot_prev, rot[:-1]], axis=0)
            xs = jnp.where(t8 < s, rot_prev, rot)
        term = xs * w[k:k + 1][None]
        y = term if y is None else y + term
    return (y + b[None]).reshape(bb, tt, c)


def _tile_scan(a3, b3):
    shape = a3.shape
    c = shape[-1]
    a3 = a3.reshape(-1, SUBLANES, c)
    b3 = b3.reshape(-1, SUBLANES, c)
    t8 = lax.broadcasted_iota(jnp.int32, (1, SUBLANES, c), 1)
    for s in (1, 2, 4):
        keep = t8 >= s
        a_sh = jnp.where(keep, pltpu.roll(a3, s, axis=1), 1.0)
        b_sh = jnp.where(keep, pltpu.roll(b3, s, axis=1), 0.0)
        b3 = a3 * b_sh + b3
        a3 = a3 * a_sh
    return a3.reshape(shape), b3.reshape(shape)


def _mod_kernel(c_ref, w_ref, b_ref, o_ref):
    c = c_ref[...]
    act = c * jax.nn.sigmoid(c)
    o_ref[0] = _dot(act.astype(BF16), w_ref[0].astype(BF16)) + b_ref[0]


def _rglru_kernel(x_ref, g_ref, sh_ref, sc_ref, gt_ref, hist0_ref, h0_ref, win_ref, cw_ref, cb_ref,
                  wa_ref, wx_ref, ba_ref, bx_ref, lam_ref, wout_ref,
                  xo_ref, tail_ref, hlast_ref):
    bb, tt, d = x_ref.shape
    c = D_RNN
    r = bb * tt
    hw = RG_HALF

    @pl.when(pl.program_id(1) == 0)
    def _():
        tail_ref[...] = hist0_ref[...]
        hlast_ref[...] = h0_ref[...]

    x = x_ref[...]
    hnb = _modnorm(x, g_ref[...][None], sh_ref[...], sc_ref[...]).reshape(r, d).astype(BF16)

    def project(h):
        return (_dot(hnb, win_ref[0, :, h * hw:(h + 1) * hw]),
                _dot(hnb, win_ref[0, :, c + h * hw:c + (h + 1) * hw]))

    def mix(h, gate, xb):
        cols = slice(h * hw, (h + 1) * hw)
        xb3 = xb.reshape(bb, tt, hw)
        xc3 = _causal_conv(xb3, tail_ref[:, :, cols], cw_ref[:, cols], cb_ref[:, cols])
        tail_ref[:, :, cols] = xb3[:, tt - SUBLANES:, :]
        xc = xc3.reshape(r, hw)
        xcb = xc.astype(BF16)
        rg = jax.nn.sigmoid(_dot(xcb, wa_ref[h]) + ba_ref[:, cols])
        ig = jax.nn.sigmoid(_dot(xcb, wx_ref[h]) + bx_ref[:, cols])
        log_a = (-RG_C * rg) * _softplus(-lam_ref[:, cols])
        a = jnp.exp(log_a)
        y = 1.0 - a * a
        root = jnp.where(y > 0.0, y * lax.rsqrt(y), 0.0)
        bv = root * (ig * xc)
        a3, b3 = _tile_scan(a.reshape(bb, tt, hw), bv.reshape(bb, tt, hw))
        if tt == SUBLANES:
            hs3 = a3 * hlast_ref[:, :, cols] + b3
            hlast_ref[:, :, cols] = hs3[:, tt - 1:tt, :]
            hs = hs3.reshape(r, hw)
        else:
            hrow = hlast_ref[0, :, cols]
            tiles = []
            for j in range(r // SUBLANES):
                rows = slice(j * SUBLANES, (j + 1) * SUBLANES)
                hs_t = a3[0, rows] * hrow + b3[0, rows]
                tiles.append(hs_t)
                hrow = hs_t[SUBLANES - 1:SUBLANES, :]
            hlast_ref[0, :, cols] = hrow
            hs = jnp.concatenate(tiles, axis=0)
        z = (jax.nn.gelu(gate) * hs).astype(BF16)
        return _dot(z, wout_ref[0, cols, :])

    halves = [project(0), project(1)]
    y = mix(0, *halves[0]) + mix(1, *halves[1])
    xo_ref[...] = x + (1.0 + gt_ref[...]) * y.reshape(bb, tt, d)


def _ffn_kernel(*refs, final):
    if final:
        (x_ref, g_ref, sh_ref, sc_ref, gt_ref, hist0_ref, wup_ref, cw_ref, cb_ref, wdn_ref, gf_ref,
         xo_ref, tail_ref) = refs
    else:
        (x_ref, g_ref, sh_ref, sc_ref, gt_ref, hist0_ref, wup_ref, cw_ref, cb_ref, wdn_ref,
         xo_ref, tail_ref) = refs
    bb, tt, d = x_ref.shape
    r = bb * tt

    @pl.when(pl.program_id(1) == 0)
    def _():
        tail_ref[...] = hist0_ref[...]

    x = x_ref[...]
    hn = _modnorm(x, g_ref[...][None], sh_ref[...], sc_ref[...]).reshape(r, d).astype(BF16)
    n_chunks = D_FF // FFN_CHUNK

    def up(n):
        return [_dot(hn, wup_ref[0, :, lo:lo + FFN_CHUNK]).reshape(bb, tt, FFN_CHUNK)
                for lo in (n * FFN_CHUNK, D_FF + n * FFN_CHUNK)]

    acc = jnp.zeros((r, d), F32)
    u_next = up(0)
    for n in range(n_chunks):
        u_cur = u_next
        if n + 1 < n_chunks:
            u_next = up(n + 1)
        parts = []
        for u3, lo in zip(u_cur, (n * FFN_CHUNK, D_FF + n * FFN_CHUNK)):
            cols = slice(lo, lo + FFN_CHUNK)
            parts.append(_causal_conv(u3, tail_ref[:, :, cols], cw_ref[:, cols], cb_ref[:, cols]))
            tail_ref[:, :, cols] = u3[:, tt - SUBLANES:, :]
        z = (jax.nn.gelu(parts[0]) * parts[1]).reshape(r, FFN_CHUNK).astype(BF16)
        acc = acc + _dot(z, wdn_ref[0, n * FFN_CHUNK:(n + 1) * FFN_CHUNK, :])
    xn = x + (1.0 + gt_ref[...]) * acc.reshape(bb, tt, d)
    if final:
        xn = _rmsnorm(xn, gf_ref[...][None])
    xo_ref[...] = xn


def _split3(x):
    hi = x.astype(BF16).astype(F32)
    r1 = x - hi
    mid = r1.astype(BF16).astype(F32)
    lo = (r1 - mid).astype(BF16).astype(F32)
    return hi, mid, lo


def _kv_kernel(*refs, prompt):
    if prompt:
        (x_ref, g_ref, sh_ref, sc_ref, wk_ref, wv_ref, wf_ref, bf_ref, cum0_ref, sel_ref,
         k_ref, v_ref, lf_ref, kaug_ref, vt_ref, cumt_ref, carry) = refs
    else:
        (x_ref, g_ref, sh_ref, sc_ref, wk_ref, wv_ref, wf_ref, bf_ref, cum0_ref,
         k_ref, v_ref, lf_ref, cum_ref, carry) = refs
    bb, tt, d = x_ref.shape
    r = bb * tt

    @pl.when(pl.program_id(1) == 0)
    def _():
        carry[...] = cum0_ref[...]

    hk = _modnorm(x_ref[...], g_ref[...][None], sh_ref[...], sc_ref[...]).reshape(r, d).astype(BF16)
    k = _dot(hk, wk_ref[...])
    v = _dot(hk, wv_ref[...])
    lf3 = _log_sigmoid(_dot(hk, wf_ref[...]) + bf_ref[...]).reshape(bb, tt, LANES)
    k_ref[...] = k.reshape(bb, tt, HW)
    v_ref[...] = v.reshape(bb, tt, HW)
    lf_ref[...] = lf3
    tpos = lax.broadcasted_iota(jnp.int32, (1, tt, LANES), 1)
    cs = lf3
    s = 1
    while s < tt:
        cs = cs + jnp.where(tpos >= s, pltpu.roll(cs, s, axis=1), 0.0)
        s *= 2
    cum = cs + carry[...]
    carry[...] = cum[:, tt - 1:tt, :]
    if not prompt:
        cum_ref[...] = cum
        return
    ck = cum[0] * LOG2E
    pieces = jnp.concatenate(_split3(ck), axis=1).astype(BF16)
    kx = _dot(pieces, sel_ref[...])
    lane = lax.broadcasted_iota(jnp.int32, (1, kx.shape[1]), 1) % LANES
    kx = jnp.where((lane >= 2 * N_SPLIT) & (lane < 3 * N_SPLIT), 1.0, kx).astype(BF16)
    kb = k.astype(BF16)
    for p in range(N_HEADS // 2):
        kaug_ref[0, :, 2 * p * LANES:(2 * p + 1) * LANES] = kb[:, p * LANES:(p + 1) * LANES]
        kaug_ref[0, :, (2 * p + 1) * LANES:(2 * p + 2) * LANES] = kx[:, p * LANES:(p + 1) * LANES]
    vt_ref[0] = v.T.astype(BF16)
    cumt_ref[0] = ck.T[:N_HEADS, :]


def _qproj_kernel(x_ref, g_ref, sh_ref, sc_ref, wq_ref, q_ref, *, transposed):
    bb, tt, d = x_ref.shape
    hn = _modnorm(x_ref[...], g_ref[...][None], sh_ref[...], sc_ref[...])
    q = _dot(hn.reshape(bb * tt, d).astype(BF16), wq_ref[0]) * (HEAD_DIM ** -0.5)
    if transposed:
        q_ref[0] = (q * LOG2E).T.astype(q_ref.dtype)
    else:
        q_ref[...] = q.reshape(bb, tt, HW).astype(q_ref.dtype)


def _oproj_kernel(x_ref, o_ref, gt_ref, wo_ref, xo_ref, *, transposed):
    bb, tt, d = x_ref.shape
    if transposed:
        y = lax.dot_general(o_ref[0], wo_ref[0], (((0,), (0,)), ((), ())), preferred_element_type=F32)
    else:
        y = _dot(o_ref[...].reshape(bb * tt, HW).astype(BF16), wo_ref[0])
    xo_ref[...] = x_ref[...] + (1.0 + gt_ref[...]) * y.reshape(bb, tt, d)


def _flash_kernel(qi_ref, ki_ref, kaug_ref, qt_ref, vt_ref, cumt_ref, o_ref, qa_s, m_s, acc_s):
    group = pl.program_id(1)
    step_id = pl.program_id(2)
    qi = qi_ref[step_id]
    ki = ki_ref[step_id]
    n_hp, _, tq = qa_s.shape
    tk = kaug_ref.shape[1]
    row = lax.broadcasted_iota(jnp.int32, (LANES, 1), 0)

    @pl.when(ki == 0)
    def _():
        m_s[...] = jnp.full(m_s.shape, -jnp.inf, F32)
        acc_s[...] = jnp.zeros(acc_s.shape, F32)
        for hh in range(n_hp):
            pr, j = divmod(hh, 2)
            qt = qt_ref[0, pr * LANES:(pr + 1) * LANES, :]
            in_head = (row >= j * HEAD_DIM) & (row < (j + 1) * HEAD_DIM)
            qa_s[hh, :LANES, :] = jnp.where(in_head, qt, jnp.zeros_like(qt))
            hi, mid, lo = _split3(cumt_ref[0, pl.ds(group * n_hp + hh, 1), :])
            pick = ((row >= j * N_SPLIT) & (row < (j + 1) * N_SPLIT)).astype(F32)
            ex = jnp.where(row == 2 * N_SPLIT, hi,
                           jnp.where(row == 2 * N_SPLIT + 1, mid,
                                     jnp.where(row == 2 * N_SPLIT + 2, lo, pick)))
            qa_s[hh, LANES:, :] = ex.astype(BF16)

    def step(masked):
        scores = []
        for hh in range(n_hp):
            pr, j = divmod(hh, 2)
            ka = kaug_ref[0, :, 2 * pr * LANES:(2 * pr + 2) * LANES]
            s = _dot(ka, qa_s[hh])
            if masked:
                kpos = lax.broadcasted_iota(jnp.int32, (tk, tq), 0)
                qpos = lax.broadcasted_iota(jnp.int32, (tk, tq), 1)
                s = jnp.where(kpos <= qpos, s, -jnp.inf)
            scores.append(s)
        probs = []
        for hh in range(n_hp):
            s = scores[hh]
            m_prev = m_s[hh]
            m_new = jnp.maximum(m_prev, jnp.max(s, axis=0, keepdims=True))
            probs.append((jnp.exp2(m_prev - m_new), jnp.exp2(s - m_new).astype(BF16)))
            m_s[hh] = m_new
        ones = jnp.ones((acc_s.shape[1] - HEAD_DIM, tk), BF16)
        for hh in range(n_hp):
            alpha, p = probs[hh]
            vta = jnp.concatenate([vt_ref[0, hh * HEAD_DIM:(hh + 1) * HEAD_DIM, :], ones], axis=0)
            acc_s[hh] = alpha * acc_s[hh] + _dot(vta, p)

    @pl.when(ki < qi)
    def _():
        step(False)

    @pl.when(ki == qi)
    def _():
        step(True)
        for hh in range(n_hp):
            acc = acc_s[hh]
            o = acc[:HEAD_DIM] / acc[HEAD_DIM:HEAD_DIM + 1]
            o_ref[0, hh * HEAD_DIM:(hh + 1) * HEAD_DIM, :] = o.astype(o_ref.dtype)


def _paged_cum_kernel(pt_ref, *refs):
    n_pages = len(refs) - 2
    ck_ref, tot_ref = refs[n_pages:]
    cs = jnp.concatenate([refs[n][0] for n in range(n_pages)], axis=0)
    lane = lax.broadcasted_iota(jnp.int32, cs.shape, 1)
    s = 1
    while s < PAGE_SIZE:
        cs = cs + jnp.where(lane >= s, pltpu.roll(cs, s, axis=1), 0.0)
        s *= 2
    totals = jnp.broadcast_to(cs[:, PAGE_SIZE - 1:PAGE_SIZE], cs.shape)
    carry = jnp.zeros((N_HEADS, PAGE_SIZE), F32)
    for n in range(n_pages):
        rows = slice(n * N_HEADS, (n + 1) * N_HEADS)
        ck_ref[0, :, n * PAGE_SIZE:(n + 1) * PAGE_SIZE] = cs[rows] + carry
        carry = carry + totals[rows]
    tot_ref[0] = carry


def _decode_attn_kernel(pt_ref, *refs, pages_per_step, gather):
    pps = pages_per_step
    q_ref = refs[0]
    if gather:
        kt_refs = refs[1:1 + pps]
        vt_refs = refs[1 + pps:1 + 2 * pps]
        (ck_ref, cq_ref, kn_ref, vn_ref, ckn_ref, o_ref, kc_ref, vc_ref,
         qbd_s, m_s, l_s, acc_s) = refs[1 + 2 * pps:]
    else:
        kc_ref, vc_ref, ck_ref, cq_ref, kn_ref, vn_ref, ckn_ref, o_ref, qbd_s, m_s, l_s, acc_s = refs[1:]
    j = pl.program_id(1)
    rows = qbd_s.shape[0]
    nt = rows // N_HEADS
    nk = pps * PAGE_SIZE
    head_mask = (lax.broadcasted_iota(jnp.int32, (N_HEADS, HW), 1) // HEAD_DIM
                 == lax.broadcasted_iota(jnp.int32, (N_HEADS, HW), 0))

    @pl.when(j == 0)
    def _():
        m_s[...] = jnp.full(m_s.shape, -jnp.inf, F32)
        l_s[...] = jnp.zeros(l_s.shape, F32)
        acc_s[...] = jnp.zeros(acc_s.shape, F32)
        q = q_ref[0]
        qbd = jnp.where(head_mask[None], q[:, None, :], 0.0)
        qbd_s[...] = qbd.reshape(rows, HW).astype(BF16)

    def update(s, pv_fn):
        m_prev = m_s[...]
        m_new = jnp.maximum(m_prev, jnp.max(s, axis=1, keepdims=True))
        alpha = jnp.exp(m_prev - m_new)
        p = jnp.exp(s - m_new)
        l_s[...] = alpha * l_s[...] + jnp.sum(p, axis=1, keepdims=True)
        acc_s[...] = alpha * acc_s[...] + pv_fn(p.astype(BF16))
        m_s[...] = m_new

    if gather:
        kt = jnp.concatenate([r[0].astype(BF16) for r in kt_refs], axis=1)
        vt = jnp.concatenate([r[0].astype(BF16) for r in vt_refs], axis=1)
        kc_ref[0] = kt
        vc_ref[0] = vt
    else:
        kt = kc_ref[0]
        vt = vc_ref[0]
    s = _dot(qbd_s[...], kt)
    bias = cq_ref[0].reshape(nt, N_HEADS, 1) - ck_ref[0][None]
    s = (s.reshape(nt, N_HEADS, nk) + bias).reshape(rows, nk)
    update(s, lambda p: lax.dot_general(p, vt, (((1,), (1,)), ((), ())), preferred_element_type=F32))

    @pl.when(j == pl.num_programs(1) - 1)
    def _():
        pad = jnp.zeros((PAGE_SIZE - nt, HW), F32)
        kb = jnp.concatenate([kn_ref[0], pad], axis=0).astype(BF16)
        vb = jnp.concatenate([vn_ref[0], pad], axis=0).astype(BF16)
        tok = lax.broadcasted_iota(jnp.int32, (rows, PAGE_SIZE), 0) // N_HEADS
        key = lax.broadcasted_iota(jnp.int32, (rows, PAGE_SIZE), 1)
        sn = lax.dot_general(qbd_s[...], kb, (((1,), (1,)), ((), ())), preferred_element_type=F32)
        sn = jnp.where(key <= tok, sn + cq_ref[0] - ckn_ref[0], -jnp.inf)
        update(sn, lambda p: _dot(p, vb))
        o_full = (acc_s[...] / l_s[...]).reshape(nt, N_HEADS, HW)
        o_ref[0] = jnp.sum(jnp.where(head_mask[None], o_full, 0.0), axis=1)


def _resident(shape):
    nd = len(shape)
    return pl.BlockSpec(shape, lambda *_: (0,) * nd, pipeline_mode=pl.Buffered(1))


def _layer_spec(stacked):
    stack, layer = stacked
    return pl.BlockSpec((1,) + stack.shape[1:], lambda *_: (layer, 0, 0), pipeline_mode=pl.Buffered(1))


def _tok_spec(bb, tt, width):
    return pl.BlockSpec((bb, tt, width), lambda i, t: (i, t, 0))


def _row_spec(bb, rows, width):
    return pl.BlockSpec((bb, rows, width), lambda i, t: (i, 0, 0))


def _params(sem):
    return pltpu.CompilerParams(dimension_semantics=sem, vmem_limit_bytes=VMEM_LIMIT)


def _mod_call(c_all, w, b):
    n_layers, d, n = w.shape
    tn = 1536 if n % 1536 == 0 else 1024
    rows = c_all.shape[0]
    return pl.pallas_call(
        _mod_kernel,
        grid=(n_layers, n // tn),
        in_specs=[pl.BlockSpec((rows, d), lambda l, j: (0, 0)),
                  pl.BlockSpec((1, d, tn), lambda l, j: (l, 0, j)),
                  pl.BlockSpec((1, 1, tn), lambda l, j: (l, 0, j))],
        out_specs=pl.BlockSpec((1, rows, tn), lambda l, j: (l, 0, j)),
        out_shape=jax.ShapeDtypeStruct((n_layers, rows, n), F32),
        compiler_params=_params(("arbitrary", "arbitrary")),
        name="mod_proj",
    )(c_all, w, b.reshape(n_layers, 1, n))


def _rglru_call(x, mods, hist0, h0, w, bb, tt):
    b, t, d = x.shape
    c = D_RNN
    r = bb * tt
    sh, sc, gt = mods
    scratch_rows = r if tt != SUBLANES else SUBLANES
    return pl.pallas_call(
        _rglru_kernel,
        grid=(b // bb, t // tt),
        in_specs=[_tok_spec(bb, tt, d), _resident((1, d)),
                  _row_spec(bb, 1, d), _row_spec(bb, 1, d), _row_spec(bb, 1, d),
                  _row_spec(bb, SUBLANES, c), _row_spec(bb, 1, c),
                  _layer_spec(w["win"]), _resident(w["cw"].shape), _resident((1, c)),
                  _resident((2, RG_HALF, RG_HALF)), _resident((2, RG_HALF, RG_HALF)),
                  _resident((1, c)), _resident((1, c)), _resident((1, c)), _layer_spec(w["wout"])],
        out_specs=[_tok_spec(bb, tt, d), _row_spec(bb, SUBLANES, c), _row_spec(bb, 1, c)],
        out_shape=[jax.ShapeDtypeStruct((b, t, d), F32),
                   jax.ShapeDtypeStruct((b, SUBLANES, c), F32),
                   jax.ShapeDtypeStruct((b, 1, c), F32)],
        compiler_params=_params(("arbitrary", "arbitrary")),
        name="rglru_mixer",
    )(x, w["g"], sh, sc, gt, hist0, h0, w["win"][0], w["cw"], w["cb"], w["wa"], w["wx"],
      w["ba"], w["bx"], w["lam"], w["wout"][0])


def _ffn_call(x, mods, hist0, w, bb, tt, final_g=None):
    b, t, d = x.shape
    sh, sc, gt = mods
    final = final_g is not None
    in_specs = [_tok_spec(bb, tt, d), _resident((1, d)),
                _row_spec(bb, 1, d), _row_spec(bb, 1, d), _row_spec(bb, 1, d),
                _row_spec(bb, SUBLANES, 2 * D_FF),
                _layer_spec(w["wup"]), _resident(w["cw"].shape), _resident((1, 2 * D_FF)),
                _layer_spec(w["wdn"])]
    args = [x, w["g"], sh, sc, gt, hist0, w["wup"][0], w["cw"], w["cb"], w["wdn"][0]]
    if final:
        in_specs.append(_resident((1, d)))
        args.append(final_g)
    return pl.pallas_call(
        functools.partial(_ffn_kernel, final=final),
        grid=(b // bb, t // tt),
        in_specs=in_specs,
        out_specs=[_tok_spec(bb, tt, d), _row_spec(bb, SUBLANES, 2 * D_FF)],
        out_shape=[jax.ShapeDtypeStruct((b, t, d), F32),
                   jax.ShapeDtypeStruct((b, SUBLANES, 2 * D_FF), F32)],
        compiler_params=_params(("arbitrary", "arbitrary")),
        name="conv_ffn",
    )(*args)


def _kv_call(x, g, shift, scale, w, cum0, bb, tt, prompt):
    b, t, d = x.shape
    in_specs = [_tok_spec(bb, tt, d), _resident((1, d)), _row_spec(bb, 1, d), _row_spec(bb, 1, d),
                _resident((d, HW)), _resident((d, HW)), _resident((d, LANES)), _resident((1, LANES)),
                _row_spec(bb, 1, LANES)]
    out_specs = [_tok_spec(bb, tt, HW), _tok_spec(bb, tt, HW), _tok_spec(bb, tt, LANES)]
    out_shape = [jax.ShapeDtypeStruct((b, t, HW), F32), jax.ShapeDtypeStruct((b, t, HW), F32),
                 jax.ShapeDtypeStruct((b, t, LANES), F32)]
    args = [x, g, shift, scale, w["wk"], w["wv"], w["wf"], w["bf"], cum0]
    if prompt:
        in_specs.append(_resident(w["sel"].shape))
        args.append(w["sel"])
        out_specs += [_tok_spec(1, tt, 2 * HW), pl.BlockSpec((1, HW, tt), lambda i, j: (i, 0, j)),
                      pl.BlockSpec((1, N_HEADS, tt), lambda i, j: (i, 0, j))]
        out_shape += [jax.ShapeDtypeStruct((b, t, 2 * HW), BF16), jax.ShapeDtypeStruct((b, HW, t), BF16),
                      jax.ShapeDtypeStruct((b, N_HEADS, t), F32)]
    else:
        out_specs.append(_tok_spec(bb, tt, LANES))
        out_shape.append(jax.ShapeDtypeStruct((b, t, LANES), F32))
    return pl.pallas_call(
        functools.partial(_kv_kernel, prompt=prompt),
        grid=(b // bb, t // tt),
        in_specs=in_specs, out_specs=out_specs, out_shape=out_shape,
        scratch_shapes=[pltpu.VMEM((bb, 1, LANES), F32)],
        compiler_params=_params(("arbitrary", "arbitrary")),
        name="kv_proj",
    )(*args)


def _qproj_call(x, g, mods, wq, bb, tt, transposed):
    b, t, d = x.shape
    sh, sc, _ = mods
    if transposed:
        out_spec = pl.BlockSpec((1, HW, tt), lambda i, j: (i, 0, j))
        out_shape = jax.ShapeDtypeStruct((b, HW, t), BF16)
    else:
        out_spec = _tok_spec(bb, tt, HW)
        out_shape = jax.ShapeDtypeStruct((b, t, HW), F32)
    return pl.pallas_call(
        functools.partial(_qproj_kernel, transposed=transposed),
        grid=(b // bb, t // tt),
        in_specs=[_tok_spec(bb, tt, d), _resident((1, d)), _row_spec(bb, 1, d), _row_spec(bb, 1, d),
                  _layer_spec(wq)],
        out_specs=out_spec,
        out_shape=out_shape,
        compiler_params=_params(("arbitrary", "arbitrary")),
        name="q_proj",
    )(x, g, sh, sc, wq[0])


def _oproj_call(x, o, gt, wo, bb, tt, transposed):
    b, t, d = x.shape
    o_spec = pl.BlockSpec((1, HW, tt), lambda i, j: (i, 0, j)) if transposed else _tok_spec(bb, tt, HW)
    return pl.pallas_call(
        functools.partial(_oproj_kernel, transposed=transposed),
        grid=(b // bb, t // tt),
        in_specs=[_tok_spec(bb, tt, d), o_spec, _row_spec(bb, 1, d), _layer_spec(wo)],
        out_specs=_tok_spec(bb, tt, d),
        out_shape=jax.ShapeDtypeStruct((b, t, d), F32),
        compiler_params=_params(("arbitrary", "arbitrary")),
        name="o_proj",
    )(x, o, gt, wo[0])


def _flash_call(kaug, qt, vt, cumt):
    b, hw, t = qt.shape
    tq = FLASH_BLOCK
    nq = t // tq
    hp = FLASH_HEADS
    pairs = [(qi, ki) for qi in range(nq) for ki in range(qi + 1)]
    qi_tab = jnp.asarray([p[0] for p in pairs], jnp.int32)
    ki_tab = jnp.asarray([p[1] for p in pairs], jnp.int32)
    return pl.pallas_call(
        _flash_kernel,
        grid_spec=pltpu.PrefetchScalarGridSpec(
            num_scalar_prefetch=2,
            grid=(b, N_HEADS // hp, len(pairs)),
            in_specs=[pl.BlockSpec((1, tq, hp * LANES), lambda i, g, s, qi, ki: (i, ki[s], g)),
                      pl.BlockSpec((1, hp * HEAD_DIM, tq), lambda i, g, s, qi, ki: (i, g, qi[s])),
                      pl.BlockSpec((1, hp * HEAD_DIM, tq), lambda i, g, s, qi, ki: (i, g, ki[s])),
                      pl.BlockSpec((1, N_HEADS, tq), lambda i, g, s, qi, ki: (i, 0, qi[s]))],
            out_specs=pl.BlockSpec((1, hp * HEAD_DIM, tq), lambda i, g, s, qi, ki: (i, g, qi[s])),
            scratch_shapes=[pltpu.VMEM((hp, 2 * LANES, tq), BF16), pltpu.VMEM((hp, 1, tq), F32),
                            pltpu.VMEM((hp, HEAD_DIM + ONES_ROWS, tq), F32)]),
        out_shape=jax.ShapeDtypeStruct((b, hw, t), BF16),
        compiler_params=_params(("arbitrary", "arbitrary", "arbitrary")),
        name="fox_prefill",
    )(qi_tab, ki_tab, kaug, qt, vt, cumt)


def _page_spec(rows, n, per_step):
    return pl.BlockSpec((1, rows, PAGE_SIZE), lambda i, j, pt: (pt[i, j * per_step + n], 0, 0))


def _paged_cum_call(page_table, lft):
    b, n_pages = page_table.shape
    return pl.pallas_call(
        _paged_cum_kernel,
        grid_spec=pltpu.PrefetchScalarGridSpec(
            num_scalar_prefetch=1,
            grid=(b, 1),
            in_specs=[_page_spec(N_HEADS, n, n_pages) for n in range(n_pages)],
            out_specs=[pl.BlockSpec((1, N_HEADS, n_pages * PAGE_SIZE), lambda i, j, pt: (i, 0, 0)),
                       pl.BlockSpec((1, N_HEADS, PAGE_SIZE), lambda i, j, pt: (i, 0, 0))]),
        out_shape=[jax.ShapeDtypeStruct((b, N_HEADS, n_pages * PAGE_SIZE), F32),
                   jax.ShapeDtypeStruct((b, N_HEADS, PAGE_SIZE), F32)],
        compiler_params=_params(("arbitrary", "arbitrary")),
        name="paged_logf_cumsum",
    )(page_table, *([lft] * n_pages))


def _decode_attn_call(page_table, q, kv_past, ckt, cq_col, k_new, v_new, ck_new, gather):
    b, nt, _ = q.shape
    n_pages = page_table.shape[1]
    pps = DECODE_PAGES
    nk = pps * PAGE_SIZE
    rows = nt * N_HEADS
    fixed = lambda shape: pl.BlockSpec(shape, lambda i, j, pt: (i, 0, 0))
    chunk = lambda width: pl.BlockSpec((1, width, nk), lambda i, j, pt: (i, 0, j))
    out_specs = [fixed((1, nt, HW))]
    out_shape = [jax.ShapeDtypeStruct((b, nt, HW), F32)]
    if gather:
        past_specs = [_page_spec(HW, n, pps) for n in range(pps)] * 2
        past_args = [kv_past[0]] * pps + [kv_past[1]] * pps
        out_specs += [chunk(HW), chunk(HW)]
        out_shape += [jax.ShapeDtypeStruct((b, HW, n_pages * PAGE_SIZE), BF16)] * 2
    else:
        past_specs = [chunk(HW), chunk(HW)]
        past_args = list(kv_past)
    return pl.pallas_call(
        functools.partial(_decode_attn_kernel, pages_per_step=pps, gather=gather),
        grid_spec=pltpu.PrefetchScalarGridSpec(
            num_scalar_prefetch=1,
            grid=(b, n_pages // pps),
            in_specs=([fixed((1, nt, HW))] + past_specs
                      + [chunk(N_HEADS), fixed((1, rows, 1)), fixed((1, nt, HW)), fixed((1, nt, HW)),
                         fixed((1, rows, PAGE_SIZE))]),
            out_specs=out_specs,
            scratch_shapes=[pltpu.VMEM((rows, HW), BF16), pltpu.VMEM((rows, 1), F32),
                            pltpu.VMEM((rows, 1), F32), pltpu.VMEM((rows, HW), F32)]),
        out_shape=out_shape,
        compiler_params=_params(("arbitrary", "arbitrary")),
        name="fox_decode",
    )(page_table, q, *past_args, ckt, cq_col, k_new, v_new, ck_new)


def _block_diag_halves(w):
    per = RG_BLOCKS // 2
    eye = jnp.eye(per, dtype=w.dtype)
    w2 = w.reshape(2, per, RG_BLOCK_W, RG_BLOCK_W)
    dense = w2[:, :, :, None, :] * eye[None, :, None, :, None]
    return dense.reshape(2, RG_HALF, RG_HALF)


def _prep_weights(p):
    row = lambda a: a.reshape(1, -1)
    stacks = {k: p[k].astype(BF16) for k in ("ffn_w_up", "ffn_w_down", "rg_w_in", "rg_w_out", "fa_wq", "fa_wo")}
    layers = []
    for l in range(DEPTH):
        w = {"mix_g": row(p["norm_mix_g"][l]),
             "ffn": {"g": row(p["norm_ffn_g"][l]), "wup": (stacks["ffn_w_up"], l),
                     "cw": p["ffn_conv_w"][l], "cb": row(p["ffn_conv_b"][l]),
                     "wdn": (stacks["ffn_w_down"], l)}}
        if l < N_A_LAYERS:
            w["rg"] = {"g": w["mix_g"], "win": (stacks["rg_w_in"], l), "cw": p["rg_conv_w"][l],
                       "cb": row(p["rg_conv_b"][l]),
                       "wa": _block_diag_halves(p["rg_wa"][l]).astype(BF16),
                       "wx": _block_diag_halves(p["rg_wx"][l]).astype(BF16),
                       "ba": row(p["rg_ba"][l]), "bx": row(p["rg_bx"][l]), "lam": row(p["rg_lambda"][l]),
                       "wout": (stacks["rg_w_out"], l)}
        else:
            j = l - N_A_LAYERS
            w["wq"] = (stacks["fa_wq"], j)
            w["wo"] = (stacks["fa_wo"], j)
        layers.append(w)
    kvw = p["kv_w"]
    kv = {"g": row(p["kv_norm_g"]), "wk": kvw[:, :HW].astype(BF16), "wv": kvw[:, HW:2 * HW].astype(BF16),
          "wf": jnp.pad(kvw[:, 2 * HW:], ((0, 0), (0, LANES - N_HEADS))).astype(BF16),
          "bf": jnp.pad(p["kv_b_f"], (0, LANES - N_HEADS)).reshape(1, LANES),
          "sel": _cum_selector()}
    return layers, kv


def _cum_selector():
    piece = jnp.arange(N_SPLIT * LANES) // LANES
    head = jnp.arange(N_SPLIT * LANES) % LANES
    target = (head // 2) * LANES + N_SPLIT * (head % 2) + piece
    hit = (jnp.arange(N_HEADS // 2 * LANES)[None, :] == target[:, None]) & (head[:, None] < N_HEADS)
    return jnp.where(hit, -1.0, 0.0).astype(BF16)


def _trunk(x, mod, kv_mod, rg_conv_buf, rg_h0, ffn_buf, layers, kvw, final_g, bb, tt, past):
    bsz, t, d = x.shape
    prompt = past is None
    lead = SUBLANES - rg_conv_buf.shape[2]
    rg_hist = jnp.pad(rg_conv_buf, ((0, 0), (0, 0), (lead, 0), (0, 0)))
    lead_f = SUBLANES - ffn_buf.shape[2]
    ffn_hist = jnp.pad(ffn_buf, ((0, 0), (0, 0), (lead_f, 0), (0, 0)))
    rg_conv_new, rg_h_new, ffn_new = [], [], []
    kv_past = None if prompt else (past["ktp"], past["vtp"])
    for l in range(DEPTH):
        w = layers[l]
        m6 = [mod[l][:, i * d:(i + 1) * d].reshape(bsz, 1, d) for i in range(6)]
        mix_mods, ffn_mods = m6[:3], m6[3:]
        if l < N_A_LAYERS:
            x, tail, hl = _rglru_call(x, mix_mods, rg_hist[l], rg_h0[l].reshape(bsz, 1, D_RNN),
                                      w["rg"], bb, tt)
            rg_conv_new.append(tail[:, lead:, :])
            rg_h_new.append(hl.reshape(bsz, D_RNN))
        else:
            q = _qproj_call(x, w["mix_g"], mix_mods, w["wq"], bb, tt, prompt)
            if prompt:
                o = _flash_call(kaug, q, vt, cumt)
            else:
                first = l == N_A_LAYERS
                res = _decode_attn_call(past["page_table"], q, kv_past, past["ckt"],
                                        cq_col, k_new, v_new, ck_new_rows, first)
                if first:
                    o, kv_past = res[0], res[1:]
                else:
                    o = res[0]
            x = _oproj_call(x, o, mix_mods[2], w["wo"], bb, tt, prompt)
        fin = final_g if l == DEPTH - 1 else None
        x, ftail = _ffn_call(x, ffn_mods, ffn_hist[l], w["ffn"], bb, tt, fin)
        ffn_new.append(ftail[:, lead_f:, :])
        if l == N_A_LAYERS - 1:
            kv_shift = kv_mod[:, :d].reshape(bsz, 1, d)
            kv_scale = kv_mod[:, d:].reshape(bsz, 1, d)
            if prompt:
                cum0 = jnp.zeros((bsz, 1, LANES), F32)
                k_new, v_new, lf, kaug, vt, cumt = _kv_call(
                    x, kvw["g"], kv_shift, kv_scale, kvw, cum0, bb, tt, True)
            else:
                cum0 = jnp.pad(past["total"], ((0, 0), (0, LANES - N_HEADS))).reshape(bsz, 1, LANES)
                k_new, v_new, lf, cum = _kv_call(x, kvw["g"], kv_shift, kv_scale, kvw, cum0, bb, tt, False)
                cum_new = cum[:, :, :N_HEADS]
                cq_col = cum_new.reshape(bsz, t * N_HEADS, 1)
                ck_t = jnp.transpose(cum_new, (0, 2, 1))
                ck_t = jnp.pad(ck_t, ((0, 0), (0, 0), (0, PAGE_SIZE - t)))
                ck_new_rows = jnp.tile(ck_t, (1, t, 1))
    k4 = k_new.reshape(bsz, t, N_HEADS, HEAD_DIM)
    v4 = v_new.reshape(bsz, t, N_HEADS, HEAD_DIM)
    return (x, k4, v4, lf[:, :, :N_HEADS], jnp.stack(rg_conv_new), jnp.stack(rg_h_new), jnp.stack(ffn_new))


def kernel(x_prompt, x_sample, c_prompt, c_sample, cache_k, cache_v, cache_logf, page_table,
           state_rglru_conv, state_rglru_h, state_ffn_conv, mod_w, mod_b, norm_mix_g, norm_ffn_g,
           rg_w_in, rg_conv_w, rg_conv_b, rg_wa, rg_ba, rg_wx, rg_bx, rg_lambda, rg_w_out,
           kv_norm_g, kv_mod_w, kv_mod_b, kv_w, kv_b_f, fa_wq, fa_wo,
           ffn_w_up, ffn_conv_w, ffn_conv_b, ffn_w_down, final_norm_g):
    p = dict(norm_mix_g=norm_mix_g, norm_ffn_g=norm_ffn_g, rg_w_in=rg_w_in, rg_conv_w=rg_conv_w,
             rg_conv_b=rg_conv_b, rg_wa=rg_wa, rg_ba=rg_ba, rg_wx=rg_wx, rg_bx=rg_bx,
             rg_lambda=rg_lambda, rg_w_out=rg_w_out, kv_norm_g=kv_norm_g, kv_w=kv_w, kv_b_f=kv_b_f,
             fa_wq=fa_wq, fa_wo=fa_wo, ffn_w_up=ffn_w_up, ffn_conv_w=ffn_conv_w,
             ffn_conv_b=ffn_conv_b, ffn_w_down=ffn_w_down)
    layers, kvw = _prep_weights(p)
    final_g = final_norm_g.reshape(1, D_MODEL)

    nb_p, nb_s = x_prompt.shape[0], x_sample.shape[0]
    n_all = nb_p + nb_s
    rows = -(-n_all // SUBLANES) * SUBLANES
    c_all = jnp.pad(jnp.concatenate([c_prompt, c_sample], axis=0), ((0, rows - n_all), (0, 0)))
    mod_all = _mod_call(c_all, mod_w, mod_b)
    kv_mod_all = _mod_call(c_all, kv_mod_w[None], kv_mod_b[None])[0]

    zero_rc = jnp.zeros((N_A_LAYERS, nb_p) + state_rglru_conv.shape[2:], F32)
    zero_rh = jnp.zeros((N_A_LAYERS, nb_p, D_RNN), F32)
    zero_fc = jnp.zeros((DEPTH, nb_p) + state_ffn_conv.shape[2:], F32)
    out_p = _trunk(x_prompt, mod_all[:, :nb_p], kv_mod_all[:nb_p], zero_rc, zero_rh, zero_fc,
                   layers, kvw, final_g, 1, 512, None)

    n_phys = cache_k.shape[0]
    ktp = jnp.transpose(cache_k, (0, 2, 3, 1)).reshape(n_phys, HW, PAGE_SIZE)
    vtp = jnp.transpose(cache_v, (0, 2, 3, 1)).reshape(n_phys, HW, PAGE_SIZE)
    lft = jnp.transpose(cache_logf, (0, 2, 1))
    ckt, tot = _paged_cum_call(page_table, lft)
    past = {"page_table": page_table, "ktp": ktp, "vtp": vtp, "ckt": ckt, "total": tot[:, :, 0]}
    out_s = _trunk(x_sample, mod_all[:, nb_p:n_all], kv_mod_all[nb_p:n_all], state_rglru_conv,
                   state_rglru_h, state_ffn_conv, layers, kvw, final_g, 32, SUBLANES, past)
    return (out_p[0], out_s[0]) + out_p[1:] + out_s[1:]
```

```python
import functools

import jax
import jax.numpy as jnp
from jax import lax
from jax.experimental import pallas as pl
from jax.experimental.pallas import tpu as pltpu

D_MODEL = 1024
DEPTH = 4
N_A_LAYERS = 2
D_RNN = 1280
RG_BLOCKS = 16
RG_BLOCK_W = D_RNN // RG_BLOCKS
RG_HALF = D_RNN // 2
RG_C = 8.0
N_HEADS = 16
HEAD_DIM = 64
HW = N_HEADS * HEAD_DIM
D_FF = 3072
EPS = 1e-6
PAGE_SIZE = 128
LANES = 128
SUBLANES = 8
FFN_CHUNK = 1536
N_SPLIT = 3
LOG2E = 1.4426950408889634
ONES_ROWS = 16
FLASH_BLOCK = 512
FLASH_HEADS = 16
DECODE_PAGES = 16
VMEM_LIMIT = 56 * 1024 * 1024

F32 = jnp.float32
BF16 = jnp.bfloat16


def _rmsnorm(x, g):
    ms = jnp.mean(x * x, axis=-1, keepdims=True)
    return x * lax.rsqrt(ms + EPS) * g


def _modnorm(x, g, shift, scale):
    return _rmsnorm(x, g) * (1.0 + scale) + shift


def _softplus(z):
    return jnp.maximum(z, 0.0) + jnp.log1p(jnp.exp(-jnp.abs(z)))


def _log_sigmoid(z):
    return -_softplus(-z)


def _dot(a, b):
    return jnp.dot(a, b, preferred_element_type=F32)


def _causal_conv(x3, prev8, w, b):
    width = w.shape[0]
    bb, tt, c = x3.shape
    assert tt == SUBLANES or bb == 1
    tiles = x3.reshape(bb * tt // SUBLANES, SUBLANES, c)
    t8 = lax.broadcasted_iota(jnp.int32, (1, SUBLANES, c), 1)
    y = None
    for k in range(width):
        s = width - 1 - k
        if s == 0:
            xs = tiles
        else:
            rot = pltpu.roll(tiles, s, axis=1)
            rot_prev = pltpu.roll(prev8, s, axis=1)
            if tt != SUBLANES:
                rot_prev = jnp.concatenate([rot_prev, rot[:-1]], axis=0)
            xs = jnp.where(t8 < s, rot_prev, rot)
        term = xs * w[k:k + 1][None]
        y = term if y is None else y + term
    return (y + b[None]).reshape(bb, tt, c)


def _tile_scan(a3, b3):
    shape = a3.shape
    c = shape[-1]
    a3 = a3.reshape(-1, SUBLANES, c)
    b3 = b3.reshape(-1, SUBLANES, c)
    t8 = lax.broadcasted_iota(jnp.int32, (1, SUBLANES, c), 1)
    for s in (1, 2, 4):
        keep = t8 >= s
        a_sh = jnp.where(keep, pltpu.roll(a3, s, axis=1), 1.0)
        b_sh = jnp.where(keep, pltpu.roll(b3, s, axis=1), 0.0)
        b3 = a3 * b_sh + b3
        a3 = a3 * a_sh
    return a3.reshape(shape), b3.reshape(shape)


def _mod_kernel(c_ref, w_ref, b_ref, o_ref):
    c = c_ref[...]
    act = c * jax.nn.sigmoid(c)
    o_ref[0] = _dot(act.astype(BF16), w_ref[0].astype(BF16)) + b_ref[0]


def _rglru_kernel(x_ref, g_ref, sh_ref, sc_ref, gt_ref, hist0_ref, h0_ref, win_ref, cw_ref, cb_ref,
                  wa_ref, wx_ref, ba_ref, bx_ref, lam_ref, wout_ref,
                  xo_ref, tail_ref, hlast_ref):
    bb, tt, d = x_ref.shape
    c = D_RNN
    r = bb * tt
    hw = RG_HALF

    @pl.when(pl.program_id(1) == 0)
    def _():
        tail_ref[...] = hist0_ref[...]
        hlast_ref[...] = h0_ref[...]

    x = x_ref[...]
    hnb = _modnorm(x, g_ref[...][None], sh_ref[...], sc_ref[...]).reshape(r, d).astype(BF16)

    def project(h):
        return (_dot(hnb, win_ref[0, :, h * hw:(h + 1) * hw]),
                _dot(hnb, win_ref[0, :, c + h * hw:c + (h + 1) * hw]))

    def mix(h, gate, xb):
        cols = slice(h * hw, (h + 1) * hw)
        xb3 = xb.reshape(bb, tt, hw)
        xc3 = _causal_conv(xb3, tail_ref[:, :, cols], cw_ref[:, cols], cb_ref[:, cols])
        tail_ref[:, :, cols] = xb3[:, tt - SUBLANES:, :]
        xc = xc3.reshape(r, hw)
        xcb = xc.astype(BF16)
        rg = jax.nn.sigmoid(_dot(xcb, wa_ref[h]) + ba_ref[:, cols])
        ig = jax.nn.sigmoid(_dot(xcb, wx_ref[h]) + bx_ref[:, cols])
        log_a = (-RG_C * rg) * _softplus(-lam_ref[:, cols])
        a = jnp.exp(log_a)
        y = 1.0 - a * a
        root = jnp.where(y > 0.0, y * lax.rsqrt(y), 0.0)
        bv = root * (ig * xc)
        a3, b3 = _tile_scan(a.reshape(bb, tt, hw), bv.reshape(bb, tt, hw))
        if tt == SUBLANES:
            hs3 = a3 * hlast_ref[:, :, cols] + b3
            hlast_ref[:, :, cols] = hs3[:, tt - 1:tt, :]
            hs = hs3.reshape(r, hw)
        else:
            hrow = hlast_ref[0, :, cols]
            tiles = []
            for j in range(r // SUBLANES):
                rows = slice(j * SUBLANES, (j + 1) * SUBLANES)
                hs_t = a3[0, rows] * hrow + b3[0, rows]
                tiles.append(hs_t)
                hrow = hs_t[SUBLANES - 1:SUBLANES, :]
            hlast_ref[0, :, cols] = hrow
            hs = jnp.concatenate(tiles, axis=0)
        z = (jax.nn.gelu(gate) * hs).astype(BF16)
        return _dot(z, wout_ref[0, cols, :])

    halves = [project(0), project(1)]
    y = mix(0, *halves[0]) + mix(1, *halves[1])
    xo_ref[...] = x + (1.0 + gt_ref[...]) * y.reshape(bb, tt, d)


def _ffn_kernel(*refs, final):
    if final:
        (x_ref, g_ref, sh_ref, sc_ref, gt_ref, hist0_ref, wup_ref, cw_ref, cb_ref, wdn_ref, gf_ref,
         xo_ref, tail_ref) = refs
    else:
        (x_ref, g_ref, sh_ref, sc_ref, gt_ref, hist0_ref, wup_ref, cw_ref, cb_ref, wdn_ref,
         xo_ref, tail_ref) = refs
    bb, tt, d = x_ref.shape
    r = bb * tt

    @pl.when(pl.program_id(1) == 0)
    def _():
        tail_ref[...] = hist0_ref[...]

    x = x_ref[...]
    hn = _modnorm(x, g_ref[...][None], sh_ref[...], sc_ref[...]).reshape(r, d).astype(BF16)
    n_chunks = D_FF // FFN_CHUNK

    def up(n):
        return [_dot(hn, wup_ref[0, :, lo:lo + FFN_CHUNK]).reshape(bb, tt, FFN_CHUNK)
                for lo in (n * FFN_CHUNK, D_FF + n * FFN_CHUNK)]

    acc = jnp.zeros((r, d), F32)
    u_next = up(0)
    for n in range(n_chunks):
        u_cur = u_next
        if n + 1 < n_chunks:
            u_next = up(n + 1)
        parts = []
        for u3, lo in zip(u_cur, (n * FFN_CHUNK, D_FF + n * FFN_CHUNK)):
            cols = slice(lo, lo + FFN_CHUNK)
            parts.append(_causal_conv(u3, tail_ref[:, :, cols], cw_ref[:, cols], cb_ref[:, cols]))
            tail_ref[:, :, cols] = u3[:, tt - SUBLANES:, :]
        z = (jax.nn.gelu(parts[0]) * parts[1]).reshape(r, FFN_CHUNK).astype(BF16)
        acc = acc + _dot(z, wdn_ref[0, n * FFN_CHUNK:(n + 1) * FFN_CHUNK, :])
    xn = x + (1.0 + gt_ref[...]) * acc.reshape(bb, tt, d)
    if final:
        xn = _rmsnorm(xn, gf_ref[...][None])
    xo_ref[...] = xn


def _split3(x):
    hi = x.astype(BF16).astype(F32)
    r1 = x - hi
    mid = r1.astype(BF16).astype(F32)
    lo = (r1 - mid).astype(BF16).astype(F32)
    return hi, mid, lo


def _kv_kernel(*refs, prompt):
    if prompt:
        (x_ref, g_ref, sh_ref, sc_ref, wk_ref, wv_ref, wf_ref, bf_ref, cum0_ref, sel_ref,
         k_ref, v_ref, lf_ref, kaug_ref, vt_ref, cumt_ref, carry) = refs
    else:
        (x_ref, g_ref, sh_ref, sc_ref, wk_ref, wv_ref, wf_ref, bf_ref, cum0_ref,
         k_ref, v_ref, lf_ref, cum_ref, carry) = refs
    bb, tt, d = x_ref.shape
    r = bb * tt

    @pl.when(pl.program_id(1) == 0)
    def _():
        carry[...] = cum0_ref[...]

    hk = _modnorm(x_ref[...], g_ref[...][None], sh_ref[...], sc_ref[...]).reshape(r, d).astype(BF16)
    k = _dot(hk, wk_ref[...])
    v = _dot(hk, wv_ref[...])
    lf3 = _log_sigmoid(_dot(hk, wf_ref[...]) + bf_ref[...]).reshape(bb, tt, LANES)
    k_ref[...] = k.reshape(bb, tt, HW)
    v_ref[...] = v.reshape(bb, tt, HW)
    lf_ref[...] = lf3
    tpos = lax.broadcasted_iota(jnp.int32, (1, tt, LANES), 1)
    cs = lf3
    s = 1
    while s < tt:
        cs = cs + jnp.where(tpos >= s, pltpu.roll(cs, s, axis=1), 0.0)
        s *= 2
    cum = cs + carry[...]
    carry[...] = cum[:, tt - 1:tt, :]
    if not prompt:
        cum_ref[...] = cum
        return
    ck = cum[0] * LOG2E
    pieces = jnp.concatenate(_split3(ck), axis=1).astype(BF16)
    kx = _dot(pieces, sel_ref[...])
    lane = lax.broadcasted_iota(jnp.int32, (1, kx.shape[1]), 1) % LANES
    kx = jnp.where((lane >= 2 * N_SPLIT) & (lane < 3 * N_SPLIT), 1.0, kx).astype(BF16)
    kb = k.astype(BF16)
    for p in range(N_HEADS // 2):
        kaug_ref[0, :, 2 * p * LANES:(2 * p + 1) * LANES] = kb[:, p * LANES:(p + 1) * LANES]
        kaug_ref[0, :, (2 * p + 1) * LANES:(2 * p + 2) * LANES] = kx[:, p * LANES:(p + 1) * LANES]
    vt_ref[0] = v.T.astype(BF16)
    cumt_ref[0] = ck.T[:N_HEADS, :]


def _qproj_kernel(x_ref, g_ref, sh_ref, sc_ref, wq_ref, q_ref, *, transposed):
    bb, tt, d = x_ref.shape
    hn = _modnorm(x_ref[...], g_ref[...][None], sh_ref[...], sc_ref[...])
    q = _dot(hn.reshape(bb * tt, d).astype(BF16), wq_ref[0]) * (HEAD_DIM ** -0.5)
    if transposed:
        q_ref[0] = (q * LOG2E).T.astype(q_ref.dtype)
    else:
        q_ref[...] = q.reshape(bb, tt, HW).astype(q_ref.dtype)


def _oproj_kernel(x_ref, o_ref, gt_ref, wo_ref, xo_ref, *, transposed):
    bb, tt, d = x_ref.shape
    if transposed:
        y = lax.dot_general(o_ref[0], wo_ref[0], (((0,), (0,)), ((), ())), preferred_element_type=F32)
    else:
        y = _dot(o_ref[...].reshape(bb * tt, HW).astype(BF16), wo_ref[0])
    xo_ref[...] = x_ref[...] + (1.0 + gt_ref[...]) * y.reshape(bb, tt, d)


def _flash_kernel(qi_ref, ki_ref, kaug_ref, qt_ref, vt_ref, cumt_ref, o_ref, qa_s, m_s, acc_s):
    group = pl.program_id(1)
    step_id = pl.program_id(2)
    qi = qi_ref[step_id]
    ki = ki_ref[step_id]
    n_hp, _, tq = qa_s.shape
    tk = kaug_ref.shape[1]
    row = lax.broadcasted_iota(jnp.int32, (LANES, 1), 0)

    @pl.when(ki == 0)
    def _():
        m_s[...] = jnp.full(m_s.shape, -jnp.inf, F32)
        acc_s[...] = jnp.zeros(acc_s.shape, F32)
        for hh in range(n_hp):
            pr, j = divmod(hh, 2)
            qt = qt_ref[0, pr * LANES:(pr + 1) * LANES, :]
            in_head = (row >= j * HEAD_DIM) & (row < (j + 1) * HEAD_DIM)
            qa_s[hh, :LANES, :] = jnp.where(in_head, qt, jnp.zeros_like(qt))
            hi, mid, lo = _split3(cumt_ref[0, pl.ds(group * n_hp + hh, 1), :])
            pick = ((row >= j * N_SPLIT) & (row < (j + 1) * N_SPLIT)).astype(F32)
            ex = jnp.where(row == 2 * N_SPLIT, hi,
                           jnp.where(row == 2 * N_SPLIT + 1, mid,
                                     jnp.where(row == 2 * N_SPLIT + 2, lo, pick)))
            qa_s[hh, LANES:, :] = ex.astype(BF16)

    def step(masked):
        scores = []
        for hh in range(n_hp):
            pr, j = divmod(hh, 2)
            ka = kaug_ref[0, :, 2 * pr * LANES:(2 * pr + 2) * LANES]
            s = _dot(ka, qa_s[hh])
            if masked:
                kpos = lax.broadcasted_iota(jnp.int32, (tk, tq), 0)
                qpos = lax.broadcasted_iota(jnp.int32, (tk, tq), 1)
                s = jnp.where(kpos <= qpos, s, -jnp.inf)
            scores.append(s)
        probs = []
        for hh in range(n_hp):
            s = scores[hh]
            m_prev = m_s[hh]
            m_new = jnp.maximum(m_prev, jnp.max(s, axis=0, keepdims=True))
            probs.append((jnp.exp2(m_prev - m_new), jnp.exp2(s - m_new).astype(BF16)))
            m_s[hh] = m_new
        ones = jnp.ones((acc_s.shape[1] - HEAD_DIM, tk), BF16)
        for hh in range(n_hp):
            alpha, p = probs[hh]
            vta = jnp.concatenate([vt_ref[0, hh * HEAD_DIM:(hh + 1) * HEAD_DIM, :], ones], axis=0)
            acc_s[hh] = alpha * acc_s[hh] + _dot(vta, p)

    @pl.when(ki < qi)
    def _():
        step(False)

    @pl.when(ki == qi)
    def _():
        step(True)
        for hh in range(n_hp):
            acc = acc_s[hh]
            o = acc[:HEAD_DIM] / acc[HEAD_DIM:HEAD_DIM + 1]
            o_ref[0, hh * HEAD_DIM:(hh + 1) * HEAD_DIM, :] = o.astype(o_ref.dtype)


def _paged_cum_kernel(pt_ref, *refs):
    n_pages = len(refs) - 2
    ck_ref, tot_ref = refs[n_pages:]
    cs = jnp.concatenate([refs[n][0] for n in range(n_pages)], axis=0)
    lane = lax.broadcasted_iota(jnp.int32, cs.shape, 1)
    s = 1
    while s < PAGE_SIZE:
        cs = cs + jnp.where(lane >= s, pltpu.roll(cs, s, axis=1), 0.0)
        s *= 2
    totals = jnp.broadcast_to(cs[:, PAGE_SIZE - 1:PAGE_SIZE], cs.shape)
    carry = jnp.zeros((N_HEADS, PAGE_SIZE), F32)
    for n in range(n_pages):
        rows = slice(n * N_HEADS, (n + 1) * N_HEADS)
        ck_ref[0, :, n * PAGE_SIZE:(n + 1) * PAGE_SIZE] = cs[rows] + carry
        carry = carry + totals[rows]
    tot_ref[0] = carry


def _decode_attn_kernel(pt_ref, *refs, pages_per_step):
    pps = pages_per_step
    q_ref = refs[0]
    kt_refs = refs[1:1 + pps]
    vt_refs = refs[1 + pps:1 + 2 * pps]
    ck_ref, cq_ref, kn_ref, vn_ref, ckn_ref, o_ref, qbd_s, m_s, l_s, acc_s = refs[1 + 2 * pps:]
    j = pl.program_id(1)
    rows = qbd_s.shape[0]
    nt = rows // N_HEADS
    nk = pps * PAGE_SIZE
    head_mask = (lax.broadcasted_iota(jnp.int32, (N_HEADS, HW), 1) // HEAD_DIM
                 == lax.broadcasted_iota(jnp.int32, (N_HEADS, HW), 0))

    @pl.when(j == 0)
    def _():
        m_s[...] = jnp.full(m_s.shape, -jnp.inf, F32)
        l_s[...] = jnp.zeros(l_s.shape, F32)
        acc_s[...] = jnp.zeros(acc_s.shape, F32)
        q = q_ref[0]
        qbd = jnp.where(head_mask[None], q[:, None, :], 0.0)
        qbd_s[...] = qbd.reshape(rows, HW).astype(BF16)

    def update(s, pv_fn):
        m_prev = m_s[...]
        m_new = jnp.maximum(m_prev, jnp.max(s, axis=1, keepdims=True))
        alpha = jnp.exp(m_prev - m_new)
        p = jnp.exp(s - m_new)
        l_s[...] = alpha * l_s[...] + jnp.sum(p, axis=1, keepdims=True)
        acc_s[...] = alpha * acc_s[...] + pv_fn(p.astype(BF16))
        m_s[...] = m_new

    kt = jnp.concatenate([r[0].astype(BF16) for r in kt_refs], axis=1)
    vt = jnp.concatenate([r[0].astype(BF16) for r in vt_refs], axis=1)
    s = _dot(qbd_s[...], kt)
    bias = cq_ref[0].reshape(nt, N_HEADS, 1) - ck_ref[0][None]
    s = (s.reshape(nt, N_HEADS, nk) + bias).reshape(rows, nk)
    update(s, lambda p: lax.dot_general(p, vt, (((1,), (1,)), ((), ())), preferred_element_type=F32))

    @pl.when(j == pl.num_programs(1) - 1)
    def _():
        pad = jnp.zeros((PAGE_SIZE - nt, HW), F32)
        kb = jnp.concatenate([kn_ref[0], pad], axis=0).astype(BF16)
        vb = jnp.concatenate([vn_ref[0], pad], axis=0).astype(BF16)
        tok = lax.broadcasted_iota(jnp.int32, (rows, PAGE_SIZE), 0) // N_HEADS
        key = lax.broadcasted_iota(jnp.int32, (rows, PAGE_SIZE), 1)
        sn = lax.dot_general(qbd_s[...], kb, (((1,), (1,)), ((), ())), preferred_element_type=F32)
        sn = jnp.where(key <= tok, sn + cq_ref[0] - ckn_ref[0], -jnp.inf)
        update(sn, lambda p: _dot(p, vb))
        o_full = (acc_s[...] / l_s[...]).reshape(nt, N_HEADS, HW)
        o_ref[0] = jnp.sum(jnp.where(head_mask[None], o_full, 0.0), axis=1)


def _resident(shape):
    nd = len(shape)
    return pl.BlockSpec(shape, lambda *_: (0,) * nd, pipeline_mode=pl.Buffered(1))


def _layer_spec(stacked):
    stack, layer = stacked
    return pl.BlockSpec((1,) + stack.shape[1:], lambda *_: (layer, 0, 0), pipeline_mode=pl.Buffered(1))


def _tok_spec(bb, tt, width):
    return pl.BlockSpec((bb, tt, width), lambda i, t: (i, t, 0))


def _row_spec(bb, rows, width):
    return pl.BlockSpec((bb, rows, width), lambda i, t: (i, 0, 0))


def _params(sem):
    return pltpu.CompilerParams(dimension_semantics=sem, vmem_limit_bytes=VMEM_LIMIT)


def _mod_call(c_all, w, b):
    n_layers, d, n = w.shape
    tn = 1536 if n % 1536 == 0 else 1024
    rows = c_all.shape[0]
    return pl.pallas_call(
        _mod_kernel,
        grid=(n_layers, n // tn),
        in_specs=[pl.BlockSpec((rows, d), lambda l, j: (0, 0)),
                  pl.BlockSpec((1, d, tn), lambda l, j: (l, 0, j)),
                  pl.BlockSpec((1, 1, tn), lambda l, j: (l, 0, j))],
        out_specs=pl.BlockSpec((1, rows, tn), lambda l, j: (l, 0, j)),
        out_shape=jax.ShapeDtypeStruct((n_layers, rows, n), F32),
        compiler_params=_params(("arbitrary", "arbitrary")),
        name="mod_proj",
    )(c_all, w, b.reshape(n_layers, 1, n))


def _rglru_call(x, mods, hist0, h0, w, bb, tt):
    b, t, d = x.shape
    c = D_RNN
    r = bb * tt
    sh, sc, gt = mods
    scratch_rows = r if tt != SUBLANES else SUBLANES
    return pl.pallas_call(
        _rglru_kernel,
        grid=(b // bb, t // tt),
        in_specs=[_tok_spec(bb, tt, d), _resident((1, d)),
                  _row_spec(bb, 1, d), _row_spec(bb, 1, d), _row_spec(bb, 1, d),
                  _row_spec(bb, SUBLANES, c), _row_spec(bb, 1, c),
                  _layer_spec(w["win"]), _resident(w["cw"].shape), _resident((1, c)),
                  _resident((2, RG_HALF, RG_HALF)), _resident((2, RG_HALF, RG_HALF)),
                  _resident((1, c)), _resident((1, c)), _resident((1, c)), _layer_spec(w["wout"])],
        out_specs=[_tok_spec(bb, tt, d), _row_spec(bb, SUBLANES, c), _row_spec(bb, 1, c)],
        out_shape=[jax.ShapeDtypeStruct((b, t, d), F32),
                   jax.ShapeDtypeStruct((b, SUBLANES, c), F32),
                   jax.ShapeDtypeStruct((b, 1, c), F32)],
        compiler_params=_params(("arbitrary", "arbitrary")),
        name="rglru_mixer",
    )(x, w["g"], sh, sc, gt, hist0, h0, w["win"][0], w["cw"], w["cb"], w["wa"], w["wx"],
      w["ba"], w["bx"], w["lam"], w["wout"][0])


def _ffn_call(x, mods, hist0, w, bb, tt, final_g=None):
    b, t, d = x.shape
    sh, sc, gt = mods
    final = final_g is not None
    in_specs = [_tok_spec(bb, tt, d), _resident((1, d)),
                _row_spec(bb, 1, d), _row_spec(bb, 1, d), _row_spec(bb, 1, d),
                _row_spec(bb, SUBLANES, 2 * D_FF),
                _layer_spec(w["wup"]), _resident(w["cw"].shape), _resident((1, 2 * D_FF)),
                _layer_spec(w["wdn"])]
    args = [x, w["g"], sh, sc, gt, hist0, w["wup"][0], w["cw"], w["cb"], w["wdn"][0]]
    if final:
        in_specs.append(_resident((1, d)))
        args.append(final_g)
    return pl.pallas_call(
        functools.partial(_ffn_kernel, final=final),
        grid=(b // bb, t // tt),
        in_specs=in_specs,
        out_specs=[_tok_spec(bb, tt, d), _row_spec(bb, SUBLANES, 2 * D_FF)],
        out_shape=[jax.ShapeDtypeStruct((b, t, d), F32),
                   jax.ShapeDtypeStruct((b, SUBLANES, 2 * D_FF), F32)],
        compiler_params=_params(("arbitrary", "arbitrary")),
        name="conv_ffn",
    )(*args)


def _kv_call(x, g, shift, scale, w, cum0, bb, tt, prompt):
    b, t, d = x.shape
    in_specs = [_tok_spec(bb, tt, d), _resident((1, d)), _row_spec(bb, 1, d), _row_spec(bb, 1, d),
                _resident((d, HW)), _resident((d, HW)), _resident((d, LANES)), _resident((1, LANES)),
                _row_spec(bb, 1, LANES)]
    out_specs = [_tok_spec(bb, tt, HW), _tok_spec(bb, tt, HW), _tok_spec(bb, tt, LANES)]
    out_shape = [jax.ShapeDtypeStruct((b, t, HW), F32), jax.ShapeDtypeStruct((b, t, HW), F32),
                 jax.ShapeDtypeStruct((b, t, LANES), F32)]
    args = [x, g, shift, scale, w["wk"], w["wv"], w["wf"], w["bf"], cum0]
    if prompt:
        in_specs.append(_resident(w["sel"].shape))
        args.append(w["sel"])
        out_specs += [_tok_spec(1, tt, 2 * HW), pl.BlockSpec((1, HW, tt), lambda i, j: (i, 0, j)),
                      pl.BlockSpec((1, N_HEADS, tt), lambda i, j: (i, 0, j))]
        out_shape += [jax.ShapeDtypeStruct((b, t, 2 * HW), BF16), jax.ShapeDtypeStruct((b, HW, t), BF16),
                      jax.ShapeDtypeStruct((b, N_HEADS, t), F32)]
    else:
        out_specs.append(_tok_spec(bb, tt, LANES))
        out_shape.append(jax.ShapeDtypeStruct((b, t, LANES), F32))
    return pl.pallas_call(
        functools.partial(_kv_kernel, prompt=prompt),
        grid=(b // bb, t // tt),
        in_specs=in_specs, out_specs=out_specs, out_shape=out_shape,
        scratch_shapes=[pltpu.VMEM((bb, 1, LANES), F32)],
        compiler_params=_params(("arbitrary", "arbitrary")),
        name="kv_proj",
    )(*args)


def _qproj_call(x, g, mods, wq, bb, tt, transposed):
    b, t, d = x.shape
    sh, sc, _ = mods
    if transposed:
        out_spec = pl.BlockSpec((1, HW, tt), lambda i, j: (i, 0, j))
        out_shape = jax.ShapeDtypeStruct((b, HW, t), BF16)
    else:
        out_spec = _tok_spec(bb, tt, HW)
        out_shape = jax.ShapeDtypeStruct((b, t, HW), F32)
    return pl.pallas_call(
        functools.partial(_qproj_kernel, transposed=transposed),
        grid=(b // bb, t // tt),
        in_specs=[_tok_spec(bb, tt, d), _resident((1, d)), _row_spec(bb, 1, d), _row_spec(bb, 1, d),
                  _layer_spec(wq)],
        out_specs=out_spec,
        out_shape=out_shape,
        compiler_params=_params(("arbitrary", "arbitrary")),
        name="q_proj",
    )(x, g, sh, sc, wq[0])


def _oproj_call(x, o, gt, wo, bb, tt, transposed):
    b, t, d = x.shape
    o_spec = pl.BlockSpec((1, HW, tt), lambda i, j: (i, 0, j)) if transposed else _tok_spec(bb, tt, HW)
    return pl.pallas_call(
        functools.partial(_oproj_kernel, transposed=transposed),
        grid=(b // bb, t // tt),
        in_specs=[_tok_spec(bb, tt, d), o_spec, _row_spec(bb, 1, d), _layer_spec(wo)],
        out_specs=_tok_spec(bb, tt, d),
        out_shape=jax.ShapeDtypeStruct((b, t, d), F32),
        compiler_params=_params(("arbitrary", "arbitrary")),
        name="o_proj",
    )(x, o, gt, wo[0])


def _flash_call(kaug, qt, vt, cumt):
    b, hw, t = qt.shape
    tq = FLASH_BLOCK
    nq = t // tq
    hp = FLASH_HEADS
    pairs = [(qi, ki) for qi in range(nq) for ki in range(qi + 1)]
    qi_tab = jnp.asarray([p[0] for p in pairs], jnp.int32)
    ki_tab = jnp.asarray([p[1] for p in pairs], jnp.int32)
    return pl.pallas_call(
        _flash_kernel,
        grid_spec=pltpu.PrefetchScalarGridSpec(
            num_scalar_prefetch=2,
            grid=(b, N_HEADS // hp, len(pairs)),
            in_specs=[pl.BlockSpec((1, tq, hp * LANES), lambda i, g, s, qi, ki: (i, ki[s], g)),
                      pl.BlockSpec((1, hp * HEAD_DIM, tq), lambda i, g, s, qi, ki: (i, g, qi[s])),
                      pl.BlockSpec((1, hp * HEAD_DIM, tq), lambda i, g, s, qi, ki: (i, g, ki[s])),
                      pl.BlockSpec((1, N_HEADS, tq), lambda i, g, s, qi, ki: (i, 0, qi[s]))],
            out_specs=pl.BlockSpec((1, hp * HEAD_DIM, tq), lambda i, g, s, qi, ki: (i, g, qi[s])),
            scratch_shapes=[pltpu.VMEM((hp, 2 * LANES, tq), BF16), pltpu.VMEM((hp, 1, tq), F32),
                            pltpu.VMEM((hp, HEAD_DIM + ONES_ROWS, tq), F32)]),
        out_shape=jax.ShapeDtypeStruct((b, hw, t), BF16),
        compiler_params=_params(("arbitrary", "arbitrary", "arbitrary")),
        name="fox_prefill",
    )(qi_tab, ki_tab, kaug, qt, vt, cumt)


def _page_spec(rows, n, per_step):
    return pl.BlockSpec((1, rows, PAGE_SIZE), lambda i, j, pt: (pt[i, j * per_step + n], 0, 0))


def _paged_cum_call(page_table, lft):
    b, n_pages = page_table.shape
    return pl.pallas_call(
        _paged_cum_kernel,
        grid_spec=pltpu.PrefetchScalarGridSpec(
            num_scalar_prefetch=1,
            grid=(b, 1),
            in_specs=[_page_spec(N_HEADS, n, n_pages) for n in range(n_pages)],
            out_specs=[pl.BlockSpec((1, N_HEADS, n_pages * PAGE_SIZE), lambda i, j, pt: (i, 0, 0)),
                       pl.BlockSpec((1, N_HEADS, PAGE_SIZE), lambda i, j, pt: (i, 0, 0))]),
        out_shape=[jax.ShapeDtypeStruct((b, N_HEADS, n_pages * PAGE_SIZE), F32),
                   jax.ShapeDtypeStruct((b, N_HEADS, PAGE_SIZE), F32)],
        compiler_params=_params(("arbitrary", "arbitrary")),
        name="paged_logf_cumsum",
    )(page_table, *([lft] * n_pages))


def _decode_attn_call(page_table, q, ktp, vtp, ckt, cq_col, k_new, v_new, ck_new):
    b, nt, _ = q.shape
    n_pages = page_table.shape[1]
    pps = DECODE_PAGES
    rows = nt * N_HEADS
    fixed = lambda shape: pl.BlockSpec(shape, lambda i, j, pt: (i, 0, 0))
    return pl.pallas_call(
        functools.partial(_decode_attn_kernel, pages_per_step=pps),
        grid_spec=pltpu.PrefetchScalarGridSpec(
            num_scalar_prefetch=1,
            grid=(b, n_pages // pps),
            in_specs=([fixed((1, nt, HW))]
                      + [_page_spec(HW, n, pps) for n in range(pps)] * 2
                      + [pl.BlockSpec((1, N_HEADS, pps * PAGE_SIZE), lambda i, j, pt: (i, 0, j)),
                         fixed((1, rows, 1)), fixed((1, nt, HW)), fixed((1, nt, HW)),
                         fixed((1, rows, PAGE_SIZE))]),
            out_specs=fixed((1, nt, HW)),
            scratch_shapes=[pltpu.VMEM((rows, HW), BF16), pltpu.VMEM((rows, 1), F32),
                            pltpu.VMEM((rows, 1), F32), pltpu.VMEM((rows, HW), F32)]),
        out_shape=jax.ShapeDtypeStruct((b, nt, HW), F32),
        compiler_params=_params(("arbitrary", "arbitrary")),
        name="fox_decode",
    )(page_table, q, *([ktp] * pps), *([vtp] * pps), ckt, cq_col, k_new, v_new, ck_new)


def _block_diag_halves(w):
    per = RG_BLOCKS // 2
    eye = jnp.eye(per, dtype=w.dtype)
    w2 = w.reshape(2, per, RG_BLOCK_W, RG_BLOCK_W)
    dense = w2[:, :, :, None, :] * eye[None, :, None, :, None]
    return dense.reshape(2, RG_HALF, RG_HALF)


def _prep_weights(p):
    row = lambda a: a.reshape(1, -1)
    stacks = {k: p[k].astype(BF16) for k in ("ffn_w_up", "ffn_w_down", "rg_w_in", "rg_w_out", "fa_wq", "fa_wo")}
    layers = []
    for l in range(DEPTH):
        w = {"mix_g": row(p["norm_mix_g"][l]),
             "ffn": {"g": row(p["norm_ffn_g"][l]), "wup": (stacks["ffn_w_up"], l),
                     "cw": p["ffn_conv_w"][l], "cb": row(p["ffn_conv_b"][l]),
                     "wdn": (stacks["ffn_w_down"], l)}}
        if l < N_A_LAYERS:
            w["rg"] = {"g": w["mix_g"], "win": (stacks["rg_w_in"], l), "cw": p["rg_conv_w"][l],
                       "cb": row(p["rg_conv_b"][l]),
                       "wa": _block_diag_halves(p["rg_wa"][l]).astype(BF16),
                       "wx": _block_diag_halves(p["rg_wx"][l]).astype(BF16),
                       "ba": row(p["rg_ba"][l]), "bx": row(p["rg_bx"][l]), "lam": row(p["rg_lambda"][l]),
                       "wout": (stacks["rg_w_out"], l)}
        else:
            j = l - N_A_LAYERS
            w["wq"] = (stacks["fa_wq"], j)
            w["wo"] = (stacks["fa_wo"], j)
        layers.append(w)
    kvw = p["kv_w"]
    kv = {"g": row(p["kv_norm_g"]), "wk": kvw[:, :HW].astype(BF16), "wv": kvw[:, HW:2 * HW].astype(BF16),
          "wf": jnp.pad(kvw[:, 2 * HW:], ((0, 0), (0, LANES - N_HEADS))).astype(BF16),
          "bf": jnp.pad(p["kv_b_f"], (0, LANES - N_HEADS)).reshape(1, LANES),
          "sel": _cum_selector()}
    return layers, kv


def _cum_selector():
    piece = jnp.arange(N_SPLIT * LANES) // LANES
    head = jnp.arange(N_SPLIT * LANES) % LANES
    target = (head // 2) * LANES + N_SPLIT * (head % 2) + piece
    hit = (jnp.arange(N_HEADS // 2 * LANES)[None, :] == target[:, None]) & (head[:, None] < N_HEADS)
    return jnp.where(hit, -1.0, 0.0).astype(BF16)


def _trunk(x, mod, kv_mod, rg_conv_buf, rg_h0, ffn_buf, layers, kvw, final_g, bb, tt, past):
    bsz, t, d = x.shape
    prompt = past is None
    lead = SUBLANES - rg_conv_buf.shape[2]
    rg_hist = jnp.pad(rg_conv_buf, ((0, 0), (0, 0), (lead, 0), (0, 0)))
    lead_f = SUBLANES - ffn_buf.shape[2]
    ffn_hist = jnp.pad(ffn_buf, ((0, 0), (0, 0), (lead_f, 0), (0, 0)))
    rg_conv_new, rg_h_new, ffn_new = [], [], []
    for l in range(DEPTH):
        w = layers[l]
        m6 = [mod[l][:, i * d:(i + 1) * d].reshape(bsz, 1, d) for i in range(6)]
        mix_mods, ffn_mods = m6[:3], m6[3:]
        if l < N_A_LAYERS:
            x, tail, hl = _rglru_call(x, mix_mods, rg_hist[l], rg_h0[l].reshape(bsz, 1, D_RNN),
                                      w["rg"], bb, tt)
            rg_conv_new.append(tail[:, lead:, :])
            rg_h_new.append(hl.reshape(bsz, D_RNN))
        else:
            q = _qproj_call(x, w["mix_g"], mix_mods, w["wq"], bb, tt, prompt)
            if prompt:
                o = _flash_call(kaug, q, vt, cumt)
            else:
                o = _decode_attn_call(past["page_table"], q, past["ktp"], past["vtp"], past["ckt"],
                                      cq_col, k_new, v_new, ck_new_rows)
            x = _oproj_call(x, o, mix_mods[2], w["wo"], bb, tt, prompt)
        fin = final_g if l == DEPTH - 1 else None
        x, ftail = _ffn_call(x, ffn_mods, ffn_hist[l], w["ffn"], bb, tt, fin)
        ffn_new.append(ftail[:, lead_f:, :])
        if l == N_A_LAYERS - 1:
            kv_shift = kv_mod[:, :d].reshape(bsz, 1, d)
            kv_scale = kv_mod[:, d:].reshape(bsz, 1, d)
            if prompt:
                cum0 = jnp.zeros((bsz, 1, LANES), F32)
                k_new, v_new, lf, kaug, vt, cumt = _kv_call(
                    x, kvw["g"], kv_shift, kv_scale, kvw, cum0, bb, tt, True)
            else:
                cum0 = jnp.pad(past["total"], ((0, 0), (0, LANES - N_HEADS))).reshape(bsz, 1, LANES)
                k_new, v_new, lf, cum = _kv_call(x, kvw["g"], kv_shift, kv_scale, kvw, cum0, bb, tt, False)
                cum_new = cum[:, :, :N_HEADS]
                cq_col = cum_new.reshape(bsz, t * N_HEADS, 1)
                ck_t = jnp.transpose(cum_new, (0, 2, 1))
                ck_t = jnp.pad(ck_t, ((0, 0), (0, 0), (0, PAGE_SIZE - t)))
                ck_new_rows = jnp.tile(ck_t, (1, t, 1))
    k4 = k_new.reshape(bsz, t, N_HEADS, HEAD_DIM)
    v4 = v_new.reshape(bsz, t, N_HEADS, HEAD_DIM)
    return (x, k4, v4, lf[:, :, :N_HEADS], jnp.stack(rg_conv_new), jnp.stack(rg_h_new), jnp.stack(ffn_new))


def kernel(x_prompt, x_sample, c_prompt, c_sample, cache_k, cache_v, cache_logf, page_table,
           state_rglru_conv, state_rglru_h, state_ffn_conv, mod_w, mod_b, norm_mix_g, norm_ffn_g,
           rg_w_in, rg_conv_w, rg_conv_b, rg_wa, rg_ba, rg_wx, rg_bx, rg_lambda, rg_w_out,
           kv_norm_g, kv_mod_w, kv_mod_b, kv_w, kv_b_f, fa_wq, fa_wo,
           ffn_w_up, ffn_conv_w, ffn_conv_b, ffn_w_down, final_norm_g):
    p = dict(norm_mix_g=norm_mix_g, norm_ffn_g=norm_ffn_g, rg_w_in=rg_w_in, rg_conv_w=rg_conv_w,
             rg_conv_b=rg_conv_b, rg_wa=rg_wa, rg_ba=rg_ba, rg_wx=rg_wx, rg_bx=rg_bx,
             rg_lambda=rg_lambda, rg_w_out=rg_w_out, kv_norm_g=kv_norm_g, kv_w=kv_w, kv_b_f=kv_b_f,
             fa_wq=fa_wq, fa_wo=fa_wo, ffn_w_up=ffn_w_up, ffn_conv_w=ffn_conv_w,
             ffn_conv_b=ffn_conv_b, ffn_w_down=ffn_w_down)
    layers, kvw = _prep_weights(p)
    final_g = final_norm_g.reshape(1, D_MODEL)

    nb_p, nb_s = x_prompt.shape[0], x_sample.shape[0]
    n_all = nb_p + nb_s
    rows = -(-n_all // SUBLANES) * SUBLANES
    c_all = jnp.pad(jnp.concatenate([c_prompt, c_sample], axis=0), ((0, rows - n_all), (0, 0)))
    mod_all = _mod_call(c_all, mod_w, mod_b)
    kv_mod_all = _mod_call(c_all, kv_mod_w[None], kv_mod_b[None])[0]

    zero_rc = jnp.zeros((N_A_LAYERS, nb_p) + state_rglru_conv.shape[2:], F32)
    zero_rh = jnp.zeros((N_A_LAYERS, nb_p, D_RNN), F32)
    zero_fc = jnp.zeros((DEPTH, nb_p) + state_ffn_conv.shape[2:], F32)
    out_p = _trunk(x_prompt, mod_all[:, :nb_p], kv_mod_all[:nb_p], zero_rc, zero_rh, zero_fc,
                   layers, kvw, final_g, 1, 512, None)

    n_phys = cache_k.shape[0]
    ktp = jnp.transpose(cache_k, (0, 2, 3, 1)).reshape(n_phys, HW, PAGE_SIZE)
    vtp = jnp.transpose(cache_v, (0, 2, 3, 1)).reshape(n_phys, HW, PAGE_SIZE)
    lft = jnp.transpose(cache_logf, (0, 2, 1))
    ckt, tot = _paged_cum_call(page_table, lft)
    past = {"page_table": page_table, "ktp": ktp, "vtp": vtp, "ckt": ckt, "total": tot[:, :, 0]}
    out_s = _trunk(x_sample, mod_all[:, nb_p:n_all], kv_mod_all[nb_p:n_all], state_rglru_conv,
                   state_rglru_h, state_ffn_conv, layers, kvw, final_g, 32, SUBLANES, past)
    return (out_p[0], out_s[0]) + out_p[1:] + out_s[1:]
```
